```python
import jax, jax.numpy as jnp
from jax import lax
import numpy as np

D_MODEL = 1024
BATCH = 1
SEQ = 16384
DEPTH = 2
DEC_BATCH = 128
DEC_SEQ = 1
PAST_LEN = 16384
PAGE_SIZE = 128

N_A_LAYERS = DEPTH // 2
N_B_LAYERS = DEPTH - N_A_LAYERS
N_DENSE = (DEPTH + 1) // 2
N_MOE = DEPTH // 2
CONV_WIDTH = 31
CONV_STATE = CONV_WIDTH - 1
N_HEADS = 16
QK_NOPE = 64
QK_ROPE = 32
V_HEAD = 64
KV_LORA = 256
Q_LORA = 384
D_FF = 2816
N_EXPERTS = 8
TOP_K = 2
ROPE_THETA = 10000.0
NORM_EPS = 1e-6
Q_BLOCK = 128
N_MOD = 6
SM_SCALE = (QK_NOPE + QK_ROPE) ** -0.5
NEG = -1e30

kernel_name = 'yoco_conformer_mla_moe_step'


def rms(x, g):
    xf = x.astype(jnp.float32)
    y = xf * lax.rsqrt(jnp.mean(xf * xf, -1, keepdims=True) + NORM_EPS)
    return (y * g.astype(jnp.float32)).astype(x.dtype)


def layer_norm(x, g, b):
    xf = x.astype(jnp.float32)
    mu = jnp.mean(xf, -1, keepdims=True)
    var = jnp.mean(jnp.square(xf - mu), -1, keepdims=True)
    y = (xf - mu) * lax.rsqrt(var + NORM_EPS)
    return (y * g.astype(jnp.float32) + b.astype(jnp.float32)).astype(x.dtype)


def ada_mod(c, w, b):
    m = jnp.einsum('bd,de->be', jax.nn.silu(c), w) + b
    return m.reshape(c.shape[0], 1, -1, c.shape[-1])


def mod_norm(x, g, shift, scale):
    return rms(x, g) * (1.0 + scale) + shift


def rope(x, pos):
    half = QK_ROPE // 2
    inv = ROPE_THETA ** (-jnp.arange(half, dtype=jnp.float32) / half)
    ang = pos.astype(jnp.float32)[:, None] * inv[None, :]
    ang = ang.reshape(ang.shape[:1] + (1,) * (x.ndim - 3) + (half,))
    cos, sin = jnp.cos(ang), jnp.sin(ang)
    xf = x.astype(jnp.float32)
    x1, x2 = xf[..., :half], xf[..., half:]
    return jnp.concatenate([x1 * cos - x2 * sin, x1 * sin + x2 * cos], -1).astype(x.dtype)


def conv_module(h, prev, w1, b1, wdw, bdw, ln_g, ln_b, w2, b2):
    a = jnp.einsum('btd,de->bte', h, w1) + b1
    u = a[..., :D_MODEL] * jax.nn.sigmoid(a[..., D_MODEL:])
    full = jnp.concatenate([prev.astype(u.dtype), u], 1)
    y = lax.conv_general_dilated(full, wdw[:, None, :].astype(full.dtype), window_strides=(1,),
                                 padding='VALID', dimension_numbers=('NWC', 'WIO', 'NWC'),
                                 feature_group_count=D_MODEL) + bdw
    y = jax.nn.silu(layer_norm(y, ln_g, ln_b))
    out = jnp.einsum('btd,de->bte', y, w2) + b2
    return out, full[:, -CONV_STATE:]


def swiglu(h, wg, wu, wd):
    g = jnp.einsum('...d,df->...f', h, wg)
    u = jnp.einsum('...d,df->...f', h, wu)
    return jnp.einsum('...f,fd->...d', jax.nn.silu(g) * u, wd)


def moe(h, wr, br, wg, wu, wd):
    logits = (jnp.einsum('btd,de->bte', h, wr) + br).astype(jnp.float32)
    top_v, top_i = lax.top_k(logits, TOP_K)
    top_w = jax.nn.softmax(top_v, -1)
    gates = jnp.sum(jax.nn.one_hot(top_i, N_EXPERTS, dtype=jnp.float32) * top_w[..., None], -2).astype(h.dtype)
    y = jnp.zeros_like(h)
    for e in range(N_EXPERTS):
        y = y + gates[..., e:e + 1] * swiglu(h, wg[e], wu[e], wd[e])
    return y


def shared_kv(x, c, pos, ada_kv_w, ada_kv_b, norm_kv_g, w_dkv, kv_norm_g, w_kr):
    m = ada_mod(c, ada_kv_w, ada_kv_b)
    h = mod_norm(x, norm_kv_g, m[:, :, 0], m[:, :, 1])
    ckv = rms(jnp.einsum('btd,dc->btc', h, w_dkv), kv_norm_g)
    kpe = rope(jnp.einsum('btd,dr->btr', h, w_kr), pos)
    return ckv, kpe


def mla_queries(h, pos, w_dq, q_norm_g, w_uq):
    cq = rms(jnp.einsum('btd,dc->btc', h, w_dq), q_norm_g)
    q = jnp.einsum('btc,chd->bthd', cq, w_uq)
    return q[..., :QK_NOPE], rope(q[..., QK_NOPE:], pos)


def mla_attend_prompt(q_nope, q_pe, ckv, kpe, w_uk, w_uv):
    B, S = ckv.shape[0], ckv.shape[1]
    k_nope = jnp.einsum('bsc,chd->bshd', ckv, w_uk)
    v = jnp.einsum('bsc,chd->bshd', ckv, w_uv)
    nb = S // Q_BLOCK
    qn = q_nope.reshape(B, nb, Q_BLOCK, N_HEADS, QK_NOPE).swapaxes(0, 1)
    qp = q_pe.reshape(B, nb, Q_BLOCK, N_HEADS, QK_ROPE).swapaxes(0, 1)
    qpos = jnp.arange(S, dtype=jnp.int32).reshape(nb, Q_BLOCK)
    kpos = jnp.arange(S, dtype=jnp.int32)

    def block(args):
        qn_b, qp_b, qpos_b = args
        s = jnp.einsum('bqhd,bkhd->bhqk', qn_b, k_nope) + jnp.einsum('bqhr,bkr->bhqk', qp_b, kpe)
        s = s.astype(jnp.float32) * SM_SCALE
        s = jnp.where(kpos[None, None, None, :] <= qpos_b[None, None, :, None], s, NEG)
        p = jax.nn.softmax(s, -1).astype(v.dtype)
        return jnp.einsum('bhqk,bkhd->bqhd', p, v)

    o = lax.map(block, (qn, qp, qpos))
    return o.swapaxes(0, 1).reshape(B, S, N_HEADS * V_HEAD)


def mla_attend_sample(q_nope, q_pe, ckv_new, kpe_new, cache_ckv, cache_kpe, page_table, w_uk, w_uv):
    Bd, T = ckv_new.shape[0], ckv_new.shape[1]
    past = page_table.shape[1] * PAGE_SIZE
    ckv_past = cache_ckv[page_table].reshape(Bd, past, KV_LORA).astype(ckv_new.dtype)
    kpe_past = cache_kpe[page_table].reshape(Bd, past, QK_ROPE).astype(kpe_new.dtype)
    q_lat = jnp.einsum('bthd,chd->bthc', q_nope, w_uk)
    s_past = jnp.einsum('bthc,bkc->bhtk', q_lat, ckv_past) + jnp.einsum('bthr,bkr->bhtk', q_pe, kpe_past)
    s_new = jnp.einsum('bthc,bkc->bhtk', q_lat, ckv_new) + jnp.einsum('bthr,bkr->bhtk', q_pe, kpe_new)
    causal = jnp.arange(T)[None, :] <= jnp.arange(T)[:, None]
    s_new = jnp.where(causal[None, None], s_new.astype(jnp.float32) * SM_SCALE, NEG)
    s = jnp.concatenate([s_past.astype(jnp.float32) * SM_SCALE, s_new], -1)
    p = jax.nn.softmax(s, -1).astype(ckv_new.dtype)
    o_lat = (jnp.einsum('bhtk,bkc->bthc', p[..., :past], ckv_past)
             + jnp.einsum('bhtk,bkc->bthc', p[..., past:], ckv_new))
    o = jnp.einsum('bthc,chd->bthd', o_lat, w_uv)
    return o.reshape(Bd, T, N_HEADS * V_HEAD)


def trunk(x, c, pos, conv_prev, attend, P):
    new_conv = []
    kv = None
    for l in range(DEPTH):
        m = ada_mod(c, P['ada_w'][l], P['ada_b'][l])
        h = mod_norm(x, P['norm_mix_g'][l], m[:, :, 0], m[:, :, 1])
        if l < N_A_LAYERS:
            y, buf = conv_module(h, conv_prev[l], P['conv_w1'][l], P['conv_b1'][l], P['conv_wdw'][l],
                                 P['conv_bdw'][l], P['conv_ln_g'][l], P['conv_ln_b'][l],
                                 P['conv_w2'][l], P['conv_b2'][l])
            new_conv.append(buf)
        else:
            j = l - N_A_LAYERS
            qn, qp = mla_queries(h, pos, P['w_dq'][j], P['q_norm_g'][j], P['w_uq'][j])
            y = jnp.einsum('bte,ed->btd', attend(qn, qp, kv[0], kv[1]), P['w_o'][j])
        x = x + m[:, :, 2] * y
        h = mod_norm(x, P['norm_ffn_g'][l], m[:, :, 3], m[:, :, 4])
        if l % 2 == 0:
            i = l // 2
            y = swiglu(h, P['ffn_w_gate'][i], P['ffn_w_up'][i], P['ffn_w_down'][i])
        else:
            i = l // 2
            y = moe(h, P['router_w'][i], P['router_b'][i], P['moe_w_gate'][i], P['moe_w_up'][i], P['moe_w_down'][i])
        x = x + m[:, :, 5] * y
        if l == N_A_LAYERS - 1:
            kv = shared_kv(x, c, pos, P['ada_kv_w'], P['ada_kv_b'], P['norm_kv_g'],
                           P['w_dkv'], P['kv_norm_g'], P['w_kr'])
    return rms(x, P['final_norm_g']), jnp.stack(new_conv), kv


def setup_inputs(seed: int = 0) -> dict:
    key = jax.random.key(seed)
    ks = iter(jax.random.split(key, 64))
    D = D_MODEL

    def nrm(shape, scale):
        return jax.random.normal(next(ks), shape, jnp.float32) * scale

    def gain(shape):
        return 1.0 + nrm(shape, 0.02)

    n_pages = PAST_LEN // PAGE_SIZE
    n_pool = (DEC_BATCH * n_pages * 5) // 4
    page_table = jax.random.permutation(next(ks), n_pool)[:DEC_BATCH * n_pages].reshape(DEC_BATCH, n_pages).astype(jnp.int32)
    return {
        'x_prompt': nrm((BATCH, SEQ, D), 1.0),
        'x_sample': nrm((DEC_BATCH, DEC_SEQ, D), 1.0),
        'c_prompt': nrm((BATCH, D), 1.0),
        'c_sample': nrm((DEC_BATCH, D), 1.0),
        'state_conv': nrm((N_A_LAYERS, DEC_BATCH, CONV_STATE, D), 0.5),
        'cache_ckv': nrm((n_pool, PAGE_SIZE, KV_LORA), 1.0),
        'cache_kpe': nrm((n_pool, PAGE_SIZE, QK_ROPE), 1.0),
        'page_table': page_table,
        'ada_w': nrm((DEPTH, D, N_MOD * D), 0.5 * D ** -0.5),
        'ada_b': nrm((DEPTH, N_MOD * D), 0.02),
        'norm_mix_g': gain((DEPTH, D)),
        'norm_ffn_g': gain((DEPTH, D)),
        'conv_w1': nrm((N_A_LAYERS, D, 2 * D), D ** -0.5),
        'conv_b1': nrm((N_A_LAYERS, 2 * D), 0.02),
        'conv_wdw': nrm((N_A_LAYERS, CONV_WIDTH, D), CONV_WIDTH ** -0.5),
        'conv_bdw': nrm((N_A_LAYERS, D), 0.02),
        'conv_ln_g': gain((N_A_LAYERS, D)),
        'conv_ln_b': nrm((N_A_LAYERS, D), 0.02),
        'conv_w2': nrm((N_A_LAYERS, D, D), D ** -0.5),
        'conv_b2': nrm((N_A_LAYERS, D), 0.02),
        'ada_kv_w': nrm((D, 2 * D), 0.5 * D ** -0.5),
        'ada_kv_b': nrm((2 * D,), 0.02),
        'norm_kv_g': gain((D,)),
        'w_dkv': nrm((D, KV_LORA), D ** -0.5),
        'kv_norm_g': gain((KV_LORA,)),
        'w_kr': nrm((D, QK_ROPE), D ** -0.5),
        'w_uk': nrm((KV_LORA, N_HEADS, QK_NOPE), KV_LORA ** -0.5),
        'w_uv': nrm((KV_LORA, N_HEADS, V_HEAD), KV_LORA ** -0.5),
        'w_dq': nrm((N_B_LAYERS, D, Q_LORA), D ** -0.5),
        'q_norm_g': gain((N_B_LAYERS, Q_LORA)),
        'w_uq': nrm((N_B_LAYERS, Q_LORA, N_HEADS, QK_NOPE + QK_ROPE), Q_LORA ** -0.5),
        'w_o': nrm((N_B_LAYERS, N_HEADS * V_HEAD, D), (N_HEADS * V_HEAD) ** -0.5),
        'ffn_w_gate': nrm((N_DENSE, D, D_FF), D ** -0.5),
        'ffn_w_up': nrm((N_DENSE, D, D_FF), D ** -0.5),
        'ffn_w_down': nrm((N_DENSE, D_FF, D), D_FF ** -0.5),
        'router_w': nrm((N_MOE, D, N_EXPERTS), D ** -0.5),
        'router_b': nrm((N_MOE, N_EXPERTS), 0.01),
        'moe_w_gate': nrm((N_MOE, N_EXPERTS, D, D_FF), D ** -0.5),
        'moe_w_up': nrm((N_MOE, N_EXPERTS, D, D_FF), D ** -0.5),
        'moe_w_down': nrm((N_MOE, N_EXPERTS, D_FF, D), D_FF ** -0.5),
        'final_norm_g': gain((D,)),
    }


def reference(x_prompt, x_sample, c_prompt, c_sample, state_conv, cache_ckv, cache_kpe, page_table,
              ada_w, ada_b, norm_mix_g, norm_ffn_g,
              conv_w1, conv_b1, conv_wdw, conv_bdw, conv_ln_g, conv_ln_b, conv_w2, conv_b2,
              ada_kv_w, ada_kv_b, norm_kv_g, w_dkv, kv_norm_g, w_kr, w_uk, w_uv,
              w_dq, q_norm_g, w_uq, w_o,
              ffn_w_gate, ffn_w_up, ffn_w_down,
              router_w, router_b, moe_w_gate, moe_w_up, moe_w_down,
              final_norm_g):
    P = {
        'ada_w': ada_w, 'ada_b': ada_b, 'norm_mix_g': norm_mix_g, 'norm_ffn_g': norm_ffn_g,
        'conv_w1': conv_w1, 'conv_b1': conv_b1, 'conv_wdw': conv_wdw, 'conv_bdw': conv_bdw,
        'conv_ln_g': conv_ln_g, 'conv_ln_b': conv_ln_b, 'conv_w2': conv_w2, 'conv_b2': conv_b2,
        'ada_kv_w': ada_kv_w, 'ada_kv_b': ada_kv_b, 'norm_kv_g': norm_kv_g, 'w_dkv': w_dkv,
        'kv_norm_g': kv_norm_g, 'w_kr': w_kr,
        'w_dq': w_dq, 'q_norm_g': q_norm_g, 'w_uq': w_uq, 'w_o': w_o,
        'ffn_w_gate': ffn_w_gate, 'ffn_w_up': ffn_w_up, 'ffn_w_down': ffn_w_down,
        'router_w': router_w, 'router_b': router_b, 'moe_w_gate': moe_w_gate,
        'moe_w_up': moe_w_up, 'moe_w_down': moe_w_down, 'final_norm_g': final_norm_g,
    }
    B, S = x_prompt.shape[0], x_prompt.shape[1]
    Bd, T = x_sample.shape[0], x_sample.shape[1]
    past = page_table.shape[1] * PAGE_SIZE

    pos_p = jnp.arange(S, dtype=jnp.int32)
    conv0 = jnp.zeros((N_A_LAYERS, B, CONV_STATE, D_MODEL), x_prompt.dtype)
    attend_p = lambda qn, qp, ckv, kpe: mla_attend_prompt(qn, qp, ckv, kpe, w_uk, w_uv)
    y_prompt, conv_state_prompt, kv_p = trunk(x_prompt, c_prompt, pos_p, conv0, attend_p, P)

    pos_s = past + jnp.arange(T, dtype=jnp.int32)
    attend_s = lambda qn, qp, ckv, kpe: mla_attend_sample(qn, qp, ckv, kpe, cache_ckv, cache_kpe,
                                                          page_table, w_uk, w_uv)
    y_sample, conv_state_sample, kv_s = trunk(x_sample, c_sample, pos_s, state_conv, attend_s, P)

    ckv_prompt = kv_p[0].reshape(B, S // PAGE_SIZE, PAGE_SIZE, KV_LORA)
    kpe_prompt = kv_p[1].reshape(B, S // PAGE_SIZE, PAGE_SIZE, QK_ROPE)
    ckv_sample = kv_s[0]
    kpe_sample = kv_s[1]
    return (y_prompt, y_sample, conv_state_prompt, conv_state_sample, ckv_prompt, kpe_prompt, ckv_sample, kpe_sample)
```

```python
import functools

import jax
import jax.numpy as jnp
from jax import lax
from jax.experimental import pallas as pl
from jax.experimental.pallas import tpu as pltpu

F32 = jnp.float32
BF16 = jnp.bfloat16

NORM_EPS = 1e-6
ROPE_THETA = 10000.0
NEG = -1e30
N_MOD = 6

V7X_VMEM_BYTES = 64 * 1024 * 1024
VMEM_LIMIT = 56 * 1024 * 1024
LANE = 128
SUBLANE = 8
HEAD_PAD = 128


def _cparams(sem):
    return pltpu.CompilerParams(dimension_semantics=sem, vmem_limit_bytes=VMEM_LIMIT)


def _rms(x, g):
    return x * lax.rsqrt(jnp.mean(x * x, -1, keepdims=True) + NORM_EPS) * g


def _modnorm(x, g, shift, scale):
    return _rms(x, g) * (1.0 + scale) + shift


def _silu(x):
    return x * jax.nn.sigmoid(x)


def _bdot(a, b):
    return jnp.dot(a, b, preferred_element_type=F32)


def _ada_body(c_ref, w_ref, b_ref, o_ref):
    s = _silu(c_ref[...]).astype(BF16)
    o_ref[...] = _bdot(s, w_ref[...].astype(BF16)) + b_ref[...]


def _ada(c_all, w, b, layer=None):
    m, d = c_all.shape
    n = w.shape[-1]
    tn = 1024
    if layer is None:
        w_spec = pl.BlockSpec((d, tn), lambda j: (0, j))
    else:
        w_spec = pl.BlockSpec((None, d, tn), lambda j: (layer, 0, j))
    return pl.pallas_call(
        _ada_body,
        grid=(n // tn,),
        in_specs=[pl.BlockSpec((m, d), lambda j: (0, 0)), w_spec,
                  pl.BlockSpec((1, tn), lambda j: (0, j))],
        out_specs=pl.BlockSpec((m, tn), lambda j: (0, j)),
        out_shape=jax.ShapeDtypeStruct((m, n), F32),
        compiler_params=_cparams(("parallel",)),
        name="ada_mod",
    )(c_all, w, b.reshape(1, n))


def _mod_spec(mods, k, d, tm):
    if mods.shape[0] == 1:
        return pl.BlockSpec((1, d), lambda i, *_: (0, k))
    return pl.BlockSpec((tm, d), lambda i, *_: (i, k))


def _const_spec(shape):
    nd = len(shape)
    return pl.BlockSpec(shape, lambda *_: (0,) * nd)


CONV_HALO = 32
CONV_TAIL = 16
CONV_RC = 32
CONV_CC = 256


def _layer_norm_silu(y, g, b):
    mu = jnp.mean(y, -1, keepdims=True)
    yc = y - mu
    var = jnp.mean(yc * yc, -1, keepdims=True)
    return _silu(yc * lax.rsqrt(var + NORM_EPS) * g + b)


def _mixer_prompt_body(x_ref, sh_ref, sc_ref, gt_ref, g_ref, w1_ref, b1_ref, wdw_ref, bdw_ref,
                       lng_ref, lnb_ref, w2_ref, b2_ref, o_ref, st_ref, ubuf, ybuf, *, tm, kw):
    d = x_ref.shape[-1]
    i = pl.program_id(0)

    @pl.when(i == 0)
    def _():
        ubuf[...] = jnp.zeros(ubuf.shape, F32)

    x = x_ref[...]
    h = _modnorm(x, g_ref[...], sh_ref[...], sc_ref[...])
    a = _bdot(h.astype(BF16), w1_ref[...]) + b1_ref[...]
    ubuf[CONV_HALO:CONV_HALO + tm, :] = a[:, :d] * jax.nn.sigmoid(a[:, d:])

    off = CONV_HALO - (kw - 1)
    nq = (kw - 1 + off) // SUBLANE + 1

    def chunk(rc, carry):
        r0 = pl.multiple_of(rc * CONV_RC, CONV_RC)
        for c0 in range(0, d, CONV_CC):
            win = ubuf[pl.ds(r0, CONV_RC + SUBLANE * (nq + 1)), c0:c0 + CONV_CC]
            y = None
            for r in range(SUBLANE):
                a_r = None
                for q in range(nq):
                    j = SUBLANE * q + r - off
                    if 0 <= j < kw:
                        t = wdw_ref[j:j + 1, c0:c0 + CONV_CC] * win[SUBLANE * q:SUBLANE * q + CONV_RC + SUBLANE]
                        a_r = t if a_r is None else a_r + t
                if a_r is not None:
                    s = a_r[r:r + CONV_RC]
                    y = s if y is None else y + s
            ybuf[pl.ds(r0, CONV_RC), c0:c0 + CONV_CC] = y + bdw_ref[:, c0:c0 + CONV_CC]
        return carry

    lax.fori_loop(0, tm // CONV_RC, chunk, 0)

    z = _layer_norm_silu(ybuf[...], lng_ref[...], lnb_ref[...])
    out = _bdot(z.astype(BF16), w2_ref[...]) + b2_ref[...]
    o_ref[...] = x + gt_ref[...] * out
    tail = ubuf[tm:tm + CONV_HALO, :]
    ubuf[0:CONV_HALO, :] = tail
    st_ref[...] = tail


def _mixer_prompt(x, mods, g, w1, b1, wdw, bdw, lng, lnb, w2, b2):
    s, d = x.shape
    kw = wdw.shape[0]
    tm = 512
    assert s % tm == 0 and kw - 1 <= CONV_HALO and s >= CONV_HALO
    body = functools.partial(_mixer_prompt_body, tm=tm, kw=kw)
    return pl.pallas_call(
        body,
        grid=(s // tm,),
        in_specs=[pl.BlockSpec((tm, d), lambda i: (i, 0)),
                  _mod_spec(mods, 0, d, tm), _mod_spec(mods, 1, d, tm), _mod_spec(mods, 2, d, tm),
                  _const_spec((1, d)), _const_spec(w1.shape), _const_spec((1, 2 * d)),
                  _const_spec(wdw.shape), _const_spec((1, d)), _const_spec((1, d)), _const_spec((1, d)),
                  _const_spec(w2.shape), _const_spec((1, d))],
        out_specs=[pl.BlockSpec((tm, d), lambda i: (i, 0)), _const_spec((CONV_HALO, d))],
        out_shape=[jax.ShapeDtypeStruct((s, d), F32), jax.ShapeDtypeStruct((CONV_HALO, d), F32)],
        scratch_shapes=[pltpu.VMEM((CONV_HALO + tm + CONV_TAIL, d), F32), pltpu.VMEM((tm, d), F32)],
        compiler_params=_cparams(("arbitrary",)),
        name="mixer_prompt",
    )(x, mods, mods, mods, g.reshape(1, d), w1, b1.reshape(1, -1), wdw, bdw.reshape(1, d),
      lng.reshape(1, d), lnb.reshape(1, d), w2, b2.reshape(1, d))


def _mixer_sample_body(x_ref, sh_ref, sc_ref, gt_ref, g_ref, w1_ref, b1_ref, st_ref, wdw_ref, bdw_ref,
                       lng_ref, lnb_ref, w2_ref, b2_ref, o_ref, u_ref, *, kw):
    d = x_ref.shape[-1]
    x = x_ref[...]
    h = _modnorm(x, g_ref[...], sh_ref[...], sc_ref[...])
    a = _bdot(h.astype(BF16), w1_ref[...]) + b1_ref[...]
    u = a[:, :d] * jax.nn.sigmoid(a[:, d:])
    u_ref[...] = u
    y = bdw_ref[...] + wdw_ref[kw - 1:kw, :] * u
    for j in range(kw - 1):
        y = y + wdw_ref[j:j + 1, :] * st_ref[j]
    z = _layer_norm_silu(y, lng_ref[...], lnb_ref[...])
    out = _bdot(z.astype(BF16), w2_ref[...]) + b2_ref[...]
    o_ref[...] = x + gt_ref[...] * out


def _mixer_sample(x, mods, g, w1, b1, state_t, wdw, bdw, lng, lnb, w2, b2):
    bsz, d = x.shape
    kw = wdw.shape[0]
    tb = 32
    body = functools.partial(_mixer_sample_body, kw=kw)
    return pl.pallas_call(
        body,
        grid=(bsz // tb,),
        in_specs=[pl.BlockSpec((tb, d), lambda i: (i, 0)),
                  _mod_spec(mods, 0, d, tb), _mod_spec(mods, 1, d, tb), _mod_spec(mods, 2, d, tb),
                  _const_spec((1, d)), _const_spec(w1.shape), _const_spec((1, 2 * d)),
                  pl.BlockSpec((kw - 1, tb, d), lambda i: (0, i, 0)),
                  _const_spec(wdw.shape), _const_spec((1, d)), _const_spec((1, d)), _const_spec((1, d)),
                  _const_spec(w2.shape), _const_spec((1, d))],
        out_specs=[pl.BlockSpec((tb, d), lambda i: (i, 0)), pl.BlockSpec((tb, d), lambda i: (i, 0))],
        out_shape=[jax.ShapeDtypeStruct((bsz, d), F32), jax.ShapeDtypeStruct((bsz, d), F32)],
        compiler_params=_cparams(("parallel",)),
        name="mixer_sample",
    )(x, mods, mods, mods, g.reshape(1, d), w1, b1.reshape(1, -1), state_t, wdw, bdw.reshape(1, d),
      lng.reshape(1, d), lnb.reshape(1, d), w2, b2.reshape(1, d))


def _route_top2(logits, n_e):
    lane = lax.broadcasted_iota(jnp.int32, logits.shape, 1)
    m1 = jnp.max(logits, -1, keepdims=True)
    i1 = jnp.min(jnp.where(logits == m1, lane, LANE), -1, keepdims=True)
    sel1 = lane == i1
    rest = jnp.where(sel1, -jnp.inf, logits)
    m2 = jnp.max(rest, -1, keepdims=True)
    i2 = jnp.min(jnp.where(rest == m2, lane, LANE), -1, keepdims=True)
    sel2 = lane == i2
    e2 = jnp.exp(m2 - m1)
    den = 1.0 + e2
    return jnp.where(sel1, 1.0 / den, 0.0) + jnp.where(sel2, e2 / den, 0.0)


def _ffn_body(*refs, moe, pre, final, f_chunks):
    refs = list(refs)
    x_ref = refs.pop(0)
    if pre:
        ao_ref, wo_ref, ga_ref = refs.pop(0), refs.pop(0), refs.pop(0)
    sh_ref, sc_ref, gt_ref, g_ref = refs.pop(0), refs.pop(0), refs.pop(0), refs.pop(0)
    if moe:
        wr_ref, br_ref = refs.pop(0), refs.pop(0)
    wg_ref, wu_ref, wd_ref = refs.pop(0), refs.pop(0), refs.pop(0)
    if final:
        fg_ref = refs.pop(0)
    o_ref = refs.pop(0)
    h_scr = refs.pop(0)
    if pre:
        xs_scr = refs.pop(0)
    if moe:
        gates_scr = refs.pop(0)

    if moe:
        e, f = pl.program_id(1), pl.program_id(2)
        first = jnp.logical_and(e == 0, f == 0)
        last = jnp.logical_and(e == pl.num_programs(1) - 1, f == pl.num_programs(2) - 1)
    else:
        f = pl.program_id(1)
        first = f == 0
        last = f == pl.num_programs(1) - 1

    @pl.when(first)
    def _():
        x = x_ref[...]
        if pre:
            x = x + ga_ref[...] * _bdot(ao_ref[...], wo_ref[...])
            xs_scr[...] = x
        h = _modnorm(x, g_ref[...], sh_ref[...], sc_ref[...])
        h_scr[...] = h.astype(BF16)
        if moe:
            logits = jnp.dot(h, wr_ref[...], precision=lax.Precision.HIGHEST,
                             preferred_element_type=F32) + br_ref[...]
            gates_scr[...] = _route_top2(logits, None)
        o_ref[...] = jnp.zeros(o_ref.shape, F32)

    hb = h_scr[...]
    acc = None
    for c0, cw in f_chunks:
        gg = _bdot(hb, wg_ref[:, c0:c0 + cw])
        uu = _bdot(hb, wu_ref[:, c0:c0 + cw])
        part = _bdot((_silu(gg) * uu).astype(BF16), wd_ref[c0:c0 + cw, :])
        acc = part if acc is None else acc + part
    if moe:
        gates = gates_scr[...]
        lane = lax.broadcasted_iota(jnp.int32, gates.shape, 1)
        acc = acc * jnp.sum(jnp.where(lane == e, gates, 0.0), -1, keepdims=True)
    o_ref[...] += acc

    @pl.when(last)
    def _():
        xb = xs_scr[...] if pre else x_ref[...]
        r = xb + gt_ref[...] * o_ref[...]
        if final:
            r = _rms(r, fg_ref[...])
        o_ref[...] = r


def _ffn(x, mods, g, wg, wu, wd, *, tm, attn=None, router=None, final_g=None):
    m, d = x.shape
    moe = router is not None
    pre = attn is not None
    final = final_g is not None
    nf_total = wg.shape[-1]
    tf = nf_total // 2
    assert tf % LANE == 0 and m % tm == 0
    f_chunks = tuple((c0, min(512, tf - c0)) for c0 in range(0, tf, 512))
    if moe:
        n_e = wg.shape[0]
        grid = (m // tm, n_e, nf_total // tf)
        wgu_spec = pl.BlockSpec((None, d, tf), lambda i, e, f: (e, 0, f))
        wd_spec = pl.BlockSpec((None, tf, d), lambda i, e, f: (e, f, 0))
        sem = ("parallel", "arbitrary", "arbitrary")
    else:
        grid = (m // tm, nf_total // tf)
        wgu_spec = pl.BlockSpec((d, tf), lambda i, f: (0, f))
        wd_spec = pl.BlockSpec((tf, d), lambda i, f: (f, 0))
        sem = ("parallel", "arbitrary")
    row_spec = pl.BlockSpec((tm, d), lambda i, *_: (i, 0))

    args, specs = [x], [row_spec]
    if pre:
        ao, wo, amods = attn
        args += [ao, wo, amods]
        specs += [row_spec, _const_spec(wo.shape), _mod_spec(amods, 2, d, tm)]
    args += [mods, mods, mods, g.reshape(1, d)]
    specs += [_mod_spec(mods, 3, d, tm), _mod_spec(mods, 4, d, tm), _mod_spec(mods, 5, d, tm),
              _const_spec((1, d))]
    if moe:
        args += list(router)
        specs += [_const_spec(router[0].shape), _const_spec(router[1].shape)]
    args += [wg, wu, wd]
    specs += [wgu_spec, wgu_spec, wd_spec]
    if final:
        args.append(final_g.reshape(1, d))
        specs.append(_const_spec((1, d)))
    scratch = [pltpu.VMEM((tm, d), BF16)]
    if pre:
        scratch.append(pltpu.VMEM((tm, d), F32))
    if moe:
        scratch.append(pltpu.VMEM((tm, LANE), F32))
    body = functools.partial(_ffn_body, moe=moe, pre=pre, final=final, f_chunks=f_chunks)
    return pl.pallas_call(
        body, grid=grid, in_specs=specs, out_specs=row_spec,
        out_shape=jax.ShapeDtypeStruct((m, d), F32),
        scratch_shapes=scratch, compiler_params=_cparams(sem),
        name="ffn_moe" if moe else "ffn_dense",
    )(*args)


def _kvq_body(*refs, emit_kv, n_heads):
    (x_ref, ksh_ref, ksc_ref, kg_ref, wdkv_ref, kvg_ref, wkr_ref, wkrr_ref, cos32_ref, sin32_ref,
     qsh_ref, qsc_ref, qg_ref, wdq_ref, qng_ref, wq_ref, wqr_ref, cosq_ref, sinq_ref) = refs[:19]
    refs = refs[19:]
    if emit_kv:
        wk_ref, ek_ref, wuv_ref = refs[:3]
        refs = refs[3:]
    ckv_o, kpe_o, q_o = refs[:3]
    x = x_ref[...]
    hk = _modnorm(x, kg_ref[...], ksh_ref[...], ksc_ref[...]).astype(BF16)
    ckv = _rms(_bdot(hk, wdkv_ref[...]), kvg_ref[...])
    ckv_o[...] = ckv
    kpe = _bdot(hk, wkr_ref[...]) * cos32_ref[...] + _bdot(hk, wkrr_ref[...]) * sin32_ref[...]
    kpe_o[...] = kpe

    hq = _modnorm(x, qg_ref[...], qsh_ref[...], qsc_ref[...]).astype(BF16)
    cq = _rms(_bdot(hq, wdq_ref[...]), qng_ref[...]).astype(BF16)
    q = _bdot(cq, wq_ref[...])
    qr = _bdot(cq, wqr_ref[...])
    cosq, sinq = cosq_ref[...], sinq_ref[...]
    for h in range(n_heads):
        sl = slice(h * HEAD_PAD, (h + 1) * HEAD_PAD)
        q_o[:, sl] = (q[:, sl] * cosq + qr[:, sl] * sinq).astype(BF16)
    if emit_kv:
        k_o, v_o = refs[3:5]
        cb = ckv.astype(BF16)
        k_o[...] = (_bdot(cb, wk_ref[...]) + _bdot(kpe.astype(BF16), ek_ref[...])).astype(BF16)
        v_o[...] = _bdot(cb, wuv_ref[...]).astype(BF16)


def _kvq(x, kv_mods, q_mods, w, tabs, *, tm, emit_kv):
    m, d = x.shape
    n_heads = w["wq"].shape[1] // HEAD_PAD
    kvl, rope = w["wdkv"].shape[1], w["wkr"].shape[1]
    row = lambda n: pl.BlockSpec((tm, n), lambda i: (i, 0))
    tab = lambda t: (pl.BlockSpec((1, t.shape[1]), lambda i: (0, 0)) if t.shape[0] == 1
                     else pl.BlockSpec((tm, t.shape[1]), lambda i: (i, 0)))
    cos32, sin32, cosq, sinq = tabs
    args = [x, kv_mods, kv_mods, w["kg"], w["wdkv"], w["kvg"], w["wkr"], w["wkrr"], cos32, sin32,
            q_mods, q_mods, w["qg"], w["wdq"], w["qng"], w["wq"], w["wqr"], cosq, sinq]
    specs = [row(d), _mod_spec(kv_mods, 0, d, tm), _mod_spec(kv_mods, 1, d, tm), _const_spec((1, d)),
             _const_spec(w["wdkv"].shape), _const_spec((1, kvl)), _const_spec(w["wkr"].shape),
             _const_spec(w["wkrr"].shape), tab(cos32), tab(sin32),
             _mod_spec(q_mods, 0, d, tm), _mod_spec(q_mods, 1, d, tm), _const_spec((1, d)),
             _const_spec(w["wdq"].shape), _const_spec(w["qng"].shape), _const_spec(w["wq"].shape),
             _const_spec(w["wqr"].shape), tab(cosq), tab(sinq)]
    out_shape = [jax.ShapeDtypeStruct((m, kvl), F32), jax.ShapeDtypeStruct((m, rope), F32),
                 jax.ShapeDtypeStruct((m, n_heads * HEAD_PAD), BF16)]
    out_specs = [row(kvl), row(rope), row(n_heads * HEAD_PAD)]
    if emit_kv:
        args += [w["wk"], w["ek"], w["wuv"]]
        specs += [_const_spec(w["wk"].shape), _const_spec(w["ek"].shape), _const_spec(w["wuv"].shape)]
        out_shape += [jax.ShapeDtypeStruct((m, n_heads * HEAD_PAD), BF16),
                      jax.ShapeDtypeStruct((m, w["wuv"].shape[1]), BF16)]
        out_specs += [row(n_heads * HEAD_PAD), row(w["wuv"].shape[1])]
    body = functools.partial(_kvq_body, emit_kv=emit_kv, n_heads=n_heads)
    return pl.pallas_call(
        body, grid=(m // tm,), in_specs=specs, out_specs=out_specs, out_shape=out_shape,
        compiler_params=_cparams(("parallel",)), name="kvq",
    )(*args)


def _flash_body(q_ref, k_ref, v_ref, o_ref, *, tq, v_head):
    qi = pl.program_id(1)
    row = lax.broadcasted_iota(jnp.int32, (tq, tq), 0)
    col = lax.broadcasted_iota(jnp.int32, (tq, tq), 1)
    causal = col <= row

    def step(j, carry, masked):
        k0 = pl.multiple_of(j * tq, tq)
        v = v_ref[pl.ds(k0, tq), :]
        new = []
        for hh in range(2):
            m, l, acc = carry[hh]
            q = q_ref[:, hh * HEAD_PAD:(hh + 1) * HEAD_PAD]
            k = k_ref[pl.ds(k0, tq), hh * HEAD_PAD:(hh + 1) * HEAD_PAD]
            s = lax.dot_general(q, k, (((1,), (1,)), ((), ())), preferred_element_type=F32)
            if masked:
                s = jnp.where(causal, s, NEG)
            m_new = jnp.maximum(m, jnp.max(s, -1, keepdims=True))
            alpha = jnp.exp(m - m_new)
            p = jnp.exp(s - m_new)
            l = alpha * l + jnp.sum(p, -1, keepdims=True)
            acc = alpha * acc + _bdot(p.astype(BF16), v)
            new.append((m_new, l, acc))
        return tuple(new)

    one = (jnp.full((tq, 1), NEG, F32), jnp.zeros((tq, 1), F32), jnp.zeros((tq, 2 * v_head), F32))
    carry = lax.fori_loop(0, qi, lambda j, c: step(j, c, False), (one, one))
    (_, l0, a0), (_, l1, a1) = step(qi, carry, True)
    lane = lax.broadcasted_iota(jnp.int32, a0.shape, 1)
    o_ref[...] = jnp.where(lane < v_head, a0 / l0, a1 / l1).astype(BF16)


def _flash_prompt(q_all, k_all, v_all, *, v_head):
    s = q_all.shape[0]
    n_pairs = q_all.shape[1] // (2 * HEAD_PAD)
    assert v_all.shape[1] == n_pairs * 2 * v_head and 2 * v_head == LANE
    tq = 512
    body = functools.partial(_flash_body, tq=tq, v_head=v_head)
    return pl.pallas_call(
        body,
        grid=(n_pairs, s // tq),
        in_specs=[pl.BlockSpec((tq, 2 * HEAD_PAD), lambda hp, qi: (qi, hp)),
                  pl.BlockSpec((s, 2 * HEAD_PAD), lambda hp, qi: (0, hp)),
                  pl.BlockSpec((s, 2 * v_head), lambda hp, qi: (0, hp))],
        out_specs=pl.BlockSpec((tq, 2 * v_head), lambda hp, qi: (qi, hp)),
        out_shape=jax.ShapeDtypeStruct((s, n_pairs * 2 * v_head), BF16),
        compiler_params=_cparams(("parallel", "arbitrary")),
        name="flash_prompt",
    )(q_all, k_all, v_all)


def _qlat_body(q_ref, w_ref, o_ref, *, n_heads, kvl):
    for h in range(n_heads):
        o_ref[:, h * kvl:(h + 1) * kvl] = _bdot(q_ref[:, h * HEAD_PAD:(h + 1) * HEAD_PAD], w_ref[h]).astype(BF16)


def _qlat(q_all, wukt):
    b = q_all.shape[0]
    n_heads, _, kvl = wukt.shape
    body = functools.partial(_qlat_body, n_heads=n_heads, kvl=kvl)
    return pl.pallas_call(
        body, grid=(1,),
        in_specs=[_const_spec(q_all.shape), _const_spec(wukt.shape)],
        out_specs=_const_spec((b, n_heads * kvl)),
        out_shape=jax.ShapeDtypeStruct((b, n_heads * kvl), BF16),
        compiler_params=_cparams(("arbitrary",)), name="q_latent",
    )(q_all, wukt)


def _ouv_body(o_ref, w_ref, out_ref, *, n_heads, kvl):
    for hp in range(n_heads // 2):
        acc = None
        for h in (2 * hp, 2 * hp + 1):
            t = _bdot(o_ref[:, h * kvl:(h + 1) * kvl].astype(BF16), w_ref[h])
            acc = t if acc is None else acc + t
        out_ref[:, hp * LANE:(hp + 1) * LANE] = acc.astype(BF16)


def _ouv(o_lat, wuvp):
    b = o_lat.shape[0]
    n_heads, kvl, _ = wuvp.shape
    body = functools.partial(_ouv_body, n_heads=n_heads, kvl=kvl)
    return pl.pallas_call(
        body, grid=(1,),
        in_specs=[_const_spec(o_lat.shape), _const_spec(wuvp.shape)],
        out_specs=_const_spec((b, n_heads // 2 * LANE)),
        out_shape=jax.ShapeDtypeStruct((b, n_heads // 2 * LANE), BF16),
        compiler_params=_cparams(("arbitrary",)), name="o_up",
    )(o_lat, wuvp)


PAGES_PER_CHUNK = 16


def _paged_body(pt_ref, ql_ref, qp_ref, cn_ref, kn_ref, cc_hbm, ck_hbm, o_ref, cbuf, kbuf, sem,
                *, n_pages, page):
    b = pl.program_id(0)
    nb = pl.num_programs(0)
    ch = PAGES_PER_CHUNK
    nch = n_pages // ch

    def copies(bb, c, slot):
        out = []
        for p in range(ch):
            pg = pt_ref[bb, c * ch + p]
            dst = pl.ds(p * page, page)
            out.append(pltpu.make_async_copy(cc_hbm.at[pg], cbuf.at[slot, dst], sem.at[0, slot]))
            out.append(pltpu.make_async_copy(ck_hbm.at[pg], kbuf.at[slot, dst], sem.at[1, slot]))
        return out

    def start(bb, c, slot):
        for cp in copies(bb, c, slot):
            cp.start()

    def wait(bb, c, slot):
        for cp in copies(bb, c, slot):
            cp.wait()

    @pl.when(b == 0)
    def _():
        start(0, 0, 0)

    ql = ql_ref[...]
    qp = qp_ref[...]
    dn = (((1,), (1,)), ((), ()))

    def attend(slot, carry):
        m, l, acc = carry
        ck = cbuf[slot].astype(BF16)
        kp = kbuf[slot].astype(BF16)
        s = (lax.dot_general(ql, ck, dn, preferred_element_type=F32)
             + lax.dot_general(qp, kp, dn, preferred_element_type=F32))
        m_new = jnp.maximum(m, jnp.max(s, -1, keepdims=True))
        alpha = jnp.exp(m - m_new)
        p = jnp.exp(s - m_new)
        l = alpha * l + jnp.sum(p, -1, keepdims=True)
        acc = alpha * acc + _bdot(p.astype(BF16), ck)
        return m_new, l, acc

    def pair(c2, carry):
        c = 2 * c2
        wait(b, c, 0)
        start(b, c + 1, 1)
        carry = attend(0, carry)
        wait(b, c + 1, 1)

        @pl.when(c + 2 < nch)
        def _():
            start(b, c + 2, 0)

        @pl.when(jnp.logical_and(c + 2 == nch, b + 1 < nb))
        def _():
            start(b + 1, 0, 0)

        return attend(1, carry)

    nh = ql.shape[0]
    init = (jnp.full((nh, 1), NEG, F32), jnp.zeros((nh, 1), F32), jnp.zeros((nh, ql.shape[1]), F32))
    m, l, acc = lax.fori_loop(0, nch // 2, pair, init)

    cn = cn_ref[...]
    s_new = (jnp.sum(ql.astype(F32) * cn, -1, keepdims=True)
             + jnp.sum(qp.astype(F32) * kn_ref[...], -1, keepdims=True))
    m_f = jnp.maximum(m, s_new)
    alpha = jnp.exp(m - m_f)
    p_new = jnp.exp(s_new - m_f)
    l = alpha * l + p_new
    o_ref[...] = (alpha * acc + p_new * cn) / l


def _paged_attention(page_table, q_lat, q_pe, ckv_new, kpe_new, cache_ckv, cache_kpe):
    bsz, nh, kvl = q_lat.shape
    rope = q_pe.shape[-1]
    n_pages = page_table.shape[1]
    page = cache_ckv.shape[1]
    assert n_pages % (2 * PAGES_PER_CHUNK) == 0
    rows = PAGES_PER_CHUNK * page
    blk = lambda a, b_: pl.BlockSpec((None, a, b_), lambda b, pt: (b, 0, 0))
    body = functools.partial(_paged_body, n_pages=n_pages, page=page)
    return pl.pallas_call(
        body,
        grid_spec=pltpu.PrefetchScalarGridSpec(
            num_scalar_prefetch=1, grid=(bsz,),
            in_specs=[blk(nh, kvl), blk(nh, rope), blk(1, kvl), blk(1, rope),
                      pl.BlockSpec(memory_space=pl.ANY), pl.BlockSpec(memory_space=pl.ANY)],
            out_specs=blk(nh, kvl),
            scratch_shapes=[pltpu.VMEM((2, rows, kvl), F32), pltpu.VMEM((2, rows, rope), F32),
                            pltpu.SemaphoreType.DMA((2, 2))]),
        out_shape=jax.ShapeDtypeStruct((bsz, nh, kvl), F32),
        compiler_params=_cparams(("arbitrary",)),
        name="paged_attention",
    )(page_table, q_lat, q_pe, ckv_new, kpe_new, cache_ckv, cache_kpe)


def _rot_half_cols(w):
    half = w.shape[-1] // 2
    return jnp.concatenate([-w[..., half:], w[..., :half]], -1)


def _rope_tables(pos, rope, nope, scale):
    half = rope // 2
    inv = ROPE_THETA ** (-jnp.arange(half, dtype=F32) / half)
    ang = pos.astype(F32)[:, None] * inv[None, :]
    cos, sin = jnp.cos(ang), jnp.sin(ang)
    cos32 = jnp.concatenate([cos, cos], -1)
    sin32 = jnp.concatenate([sin, sin], -1)
    n = pos.shape[0]
    pad = HEAD_PAD - nope - rope
    cosq = jnp.concatenate([jnp.ones((n, nope), F32), cos32, jnp.zeros((n, pad), F32)], -1) * scale
    sinq = jnp.concatenate([jnp.zeros((n, nope), F32), sin32, jnp.zeros((n, pad), F32)], -1) * scale
    return cos32, sin32, cosq, sinq


def kernel(x_prompt, x_sample, c_prompt, c_sample, state_conv, cache_ckv, cache_kpe, page_table, ada_w, ada_b, norm_mix_g, norm_ffn_g, conv_w1, conv_b1, conv_wdw, conv_bdw, conv_ln_g, conv_ln_b, conv_w2, conv_b2, ada_kv_w, ada_kv_b, norm_kv_g, w_dkv, kv_norm_g, w_kr, w_uk, w_uv, w_dq, q_norm_g, w_uq, w_o, ffn_w_gate, ffn_w_up, ffn_w_down, router_w, router_b, moe_w_gate, moe_w_up, moe_w_down, final_norm_g):
    _, s, d = x_prompt.shape
    bd = x_sample.shape[0]
    assert x_prompt.shape[0] == 1 and x_sample.shape[1] == 1 and ada_w.shape[0] == 2
    kvl, n_heads, nope = w_uk.shape
    v_head = w_uv.shape[2]
    rope = w_kr.shape[1]
    q_lora = w_dq.shape[2]
    n_e = router_w.shape[2]
    page = cache_ckv.shape[1]
    past = page_table.shape[1] * page
    pad = HEAD_PAD - nope - rope
    sm_scale = float(nope + rope) ** -0.5

    w1 = conv_w1[0].astype(BF16)
    w2 = conv_w2[0].astype(BF16)
    wg0, wu0, wd0 = ffn_w_gate[0].astype(BF16), ffn_w_up[0].astype(BF16), ffn_w_down[0].astype(BF16)
    wge, wue, wde = moe_w_gate[0].astype(BF16), moe_w_up[0].astype(BF16), moe_w_down[0].astype(BF16)
    wo = w_o[0].astype(BF16)
    wr = jnp.pad(router_w[0], ((0, 0), (0, LANE - n_e)))
    br = jnp.pad(router_b[0], (0, LANE - n_e), constant_values=-jnp.inf).reshape(1, LANE)
    wuq = w_uq[0]
    wuq_rot = jnp.concatenate([jnp.zeros_like(wuq[..., :nope]), _rot_half_cols(wuq[..., nope:])], -1)
    pad_q = lambda w: jnp.pad(w, ((0, 0), (0, 0), (0, pad))).reshape(q_lora, n_heads * HEAD_PAD).astype(BF16)
    wk = jnp.pad(w_uk, ((0, 0), (0, 0), (0, HEAD_PAD - nope))).reshape(kvl, n_heads * HEAD_PAD).astype(BF16)
    ek = jnp.tile(jnp.pad(jnp.eye(rope, dtype=F32), ((0, 0), (nope, pad))), (1, n_heads)).astype(BF16)
    kvq_w = dict(
        kg=norm_kv_g.reshape(1, d), wdkv=w_dkv.astype(BF16), kvg=kv_norm_g.reshape(1, kvl),
        wkr=w_kr.astype(BF16), wkrr=_rot_half_cols(w_kr).astype(BF16),
        qg=norm_mix_g[1].reshape(1, d), wdq=w_dq[0].astype(BF16), qng=q_norm_g[0].reshape(1, q_lora),
        wq=pad_q(wuq), wqr=pad_q(wuq_rot), wk=wk, ek=ek, wuv=w_uv.reshape(kvl, n_heads * v_head).astype(BF16))
    wukt = jnp.pad(jnp.transpose(w_uk, (1, 2, 0)), ((0, 0), (0, HEAD_PAD - nope), (0, 0))).astype(BF16)
    wuv_h = jnp.transpose(w_uv, (1, 0, 2))
    wuvp = jnp.where((jnp.arange(n_heads) % 2 == 0)[:, None, None],
                     jnp.pad(wuv_h, ((0, 0), (0, 0), (0, v_head))),
                     jnp.pad(wuv_h, ((0, 0), (0, 0), (v_head, 0)))).astype(BF16)

    n_c = bd + 1
    c_all = jnp.pad(jnp.concatenate([c_sample, c_prompt], 0), ((0, -n_c % SUBLANE), (0, 0)))
    mods0 = _ada(c_all, ada_w, ada_b[0], layer=0)
    mods1 = _ada(c_all, ada_w, ada_b[1], layer=1)
    modskv = _ada(c_all, ada_kv_w, ada_kv_b)
    mp0, mp1, mpkv = mods0[bd:n_c], mods1[bd:n_c], modskv[bd:n_c]
    ms0, ms1, mskv = mods0[:bd], mods1[:bd], modskv[:bd]

    xp = x_prompt[0]
    x1, st_p = _mixer_prompt(xp, mp0, norm_mix_g[0], w1, conv_b1[0], conv_wdw[0], conv_bdw[0],
                             conv_ln_g[0], conv_ln_b[0], w2, conv_b2[0])
    x2 = _ffn(x1, mp0, norm_ffn_g[0], wg0, wu0, wd0, tm=512)
    tabs_p = _rope_tables(jnp.arange(s, dtype=jnp.int32), rope, nope, sm_scale)
    ckv_p, kpe_p, q_p, k_p, v_p = _kvq(x2, mpkv, mp1, kvq_w, tabs_p, tm=512, emit_kv=True)
    o_p = _flash_prompt(q_p, k_p, v_p, v_head=v_head)
    y_p = _ffn(x2, mp1, norm_ffn_g[1], wge, wue, wde, tm=512, attn=(o_p, wo, mp1),
               router=(wr, br), final_g=final_norm_g)

    xs = x_sample[:, 0]
    kw = conv_wdw.shape[1]
    state_t = jnp.transpose(state_conv[0], (1, 0, 2))
    xs1, u_s = _mixer_sample(xs, ms0, norm_mix_g[0], w1, conv_b1[0], state_t, conv_wdw[0], conv_bdw[0],
                             conv_ln_g[0], conv_ln_b[0], w2, conv_b2[0])
    xs2 = _ffn(xs1, ms0, norm_ffn_g[0], wg0, wu0, wd0, tm=bd)
    tabs_s = _rope_tables(jnp.full((1,), past, jnp.int32), rope, nope, sm_scale)
    ckv_s, kpe_s, q_s = _kvq(xs2, mskv, ms1, kvq_w, tabs_s, tm=bd, emit_kv=False)
    q_lat = _qlat(q_s, wukt).reshape(bd, n_heads, kvl)
    q_pe = q_s.reshape(bd, n_heads, HEAD_PAD)[:, :, nope:nope + rope]
    o_lat = _paged_attention(page_table, q_lat, q_pe, ckv_s.reshape(bd, 1, kvl), kpe_s.reshape(bd, 1, rope),
                             cache_ckv, cache_kpe)
    o_s = _ouv(o_lat.reshape(bd, n_heads * kvl), wuvp)
    y_s = _ffn(xs2, ms1, norm_ffn_g[1], wge, wue, wde, tm=bd, attn=(o_s, wo, ms1),
               router=(wr, br), final_g=final_norm_g)

    conv_state_prompt = st_p[CONV_HALO - (kw - 1):][None, None]
    conv_state_sample = jnp.concatenate([state_conv[0][:, 1:], u_s[:, None]], 1)[None]
    return (y_p[None], y_s[:, None], conv_state_prompt, conv_state_sample,
            ckv_p.reshape(1, s // page, page, kvl), kpe_p.reshape(1, s // page, page, rope),
            ckv_s[:, None], kpe_s[:, None])
```

```python
import functools

import jax
import jax.numpy as jnp
from jax import lax
from jax.experimental import pallas as pl
from jax.experimental.pallas import tpu as pltpu

F32 = jnp.float32
BF16 = jnp.bfloat16

NORM_EPS = 1e-6
ROPE_THETA = 10000.0
NEG = -1e30
N_MOD = 6
LOG2_E = 1.4426950408889634

V7X_VMEM_BYTES = 64 * 1024 * 1024
VMEM_LIMIT = 56 * 1024 * 1024
LANE = 128
SUBLANE = 8
HEAD_PAD = 128


def _cparams(sem):
    return pltpu.CompilerParams(dimension_semantics=sem, vmem_limit_bytes=VMEM_LIMIT)


def _rms(x, g):
    return x * lax.rsqrt(jnp.mean(x * x, -1, keepdims=True) + NORM_EPS) * g


def _modnorm(x, g, shift, scale):
    return _rms(x, g) * (1.0 + scale) + shift


def _silu(x):
    return x * jax.nn.sigmoid(x)


def _bdot(a, b):
    return jnp.dot(a, b, preferred_element_type=F32)


def _ada_body(c_ref, w_ref, b_ref, o_ref):
    s = _silu(c_ref[...]).astype(BF16)
    o_ref[...] = _bdot(s, w_ref[...].astype(BF16)) + b_ref[...]


def _ada(c_all, w, b, layer=None):
    m, d = c_all.shape
    n = w.shape[-1]
    tn = 1024
    if layer is None:
        w_spec = pl.BlockSpec((d, tn), lambda j: (0, j))
    else:
        w_spec = pl.BlockSpec((None, d, tn), lambda j: (layer, 0, j))
    return pl.pallas_call(
        _ada_body,
        grid=(n // tn,),
        in_specs=[pl.BlockSpec((m, d), lambda j: (0, 0)), w_spec,
                  pl.BlockSpec((1, tn), lambda j: (0, j))],
        out_specs=pl.BlockSpec((m, tn), lambda j: (0, j)),
        out_shape=jax.ShapeDtypeStruct((m, n), F32),
        compiler_params=_cparams(("parallel",)),
        name="ada_mod",
    )(c_all, w, b.reshape(1, n))


def _mod_spec(mods, k, d, tm):
    if mods.shape[0] == 1:
        return pl.BlockSpec((1, d), lambda i, *_: (0, k))
    return pl.BlockSpec((tm, d), lambda i, *_: (i, k))


def _const_spec(shape):
    nd = len(shape)
    return pl.BlockSpec(shape, lambda *_: (0,) * nd)


CONV_HALO = 32
CONV_TAIL = 16
CONV_RC = 32
CONV_CC = 256


def _layer_norm_silu(y, g, b):
    mu = jnp.mean(y, -1, keepdims=True)
    yc = y - mu
    var = jnp.mean(yc * yc, -1, keepdims=True)
    return _silu(yc * lax.rsqrt(var + NORM_EPS) * g + b)


def _mixer_prompt_body(x_ref, sh_ref, sc_ref, gt_ref, g_ref, w1_ref, b1_ref, wdw_ref, bdw_ref,
                       lng_ref, lnb_ref, w2_ref, b2_ref, o_ref, st_ref, ubuf, ybuf, *, tm, kw):
    d = x_ref.shape[-1]
    i = pl.program_id(0)

    @pl.when(i == 0)
    def _():
        ubuf[...] = jnp.zeros(ubuf.shape, F32)

    x = x_ref[...]
    h = _modnorm(x, g_ref[...], sh_ref[...], sc_ref[...])
    a = _bdot(h.astype(BF16), w1_ref[...]) + b1_ref[...]
    ubuf[CONV_HALO:CONV_HALO + tm, :] = a[:, :d] * jax.nn.sigmoid(a[:, d:])

    off = CONV_HALO - (kw - 1)
    nq = (kw - 1 + off) // SUBLANE + 1

    def chunk(rc, carry):
        r0 = pl.multiple_of(rc * CONV_RC, CONV_RC)
        for c0 in range(0, d, CONV_CC):
            win = ubuf[pl.ds(r0, CONV_RC + SUBLANE * (nq + 1)), c0:c0 + CONV_CC]
            y = None
            for r in range(SUBLANE):
                a_r = None
                for q in range(nq):
                    j = SUBLANE * q + r - off
                    if 0 <= j < kw:
                        t = wdw_ref[j:j + 1, c0:c0 + CONV_CC] * win[SUBLANE * q:SUBLANE * q + CONV_RC + SUBLANE]
                        a_r = t if a_r is None else a_r + t
                if a_r is not None:
                    s = a_r[r:r + CONV_RC]
                    y = s if y is None else y + s
            ybuf[pl.ds(r0, CONV_RC), c0:c0 + CONV_CC] = y + bdw_ref[:, c0:c0 + CONV_CC]
        return carry

    lax.fori_loop(0, tm // CONV_RC, chunk, 0)

    z = _layer_norm_silu(ybuf[...], lng_ref[...], lnb_ref[...])
    out = _bdot(z.astype(BF16), w2_ref[...]) + b2_ref[...]
    o_ref[...] = x + gt_ref[...] * out
    tail = ubuf[tm:tm + CONV_HALO, :]
    ubuf[0:CONV_HALO, :] = tail
    st_ref[...] = tail


def _mixer_prompt(x, mods, g, w1, b1, wdw, bdw, lng, lnb, w2, b2):
    s, d = x.shape
    kw = wdw.shape[0]
    tm = 512
    assert s % tm == 0 and kw - 1 <= CONV_HALO and s >= CONV_HALO
    body = functools.partial(_mixer_prompt_body, tm=tm, kw=kw)
    return pl.pallas_call(
        body,
        grid=(s // tm,),
        in_specs=[pl.BlockSpec((tm, d), lambda i: (i, 0)),
                  _mod_spec(mods, 0, d, tm), _mod_spec(mods, 1, d, tm), _mod_spec(mods, 2, d, tm),
                  _const_spec((1, d)), _const_spec(w1.shape), _const_spec((1, 2 * d)),
                  _const_spec(wdw.shape), _const_spec((1, d)), _const_spec((1, d)), _const_spec((1, d)),
                  _const_spec(w2.shape), _const_spec((1, d))],
        out_specs=[pl.BlockSpec((tm, d), lambda i: (i, 0)), _const_spec((CONV_HALO, d))],
        out_shape=[jax.ShapeDtypeStruct((s, d), F32), jax.ShapeDtypeStruct((CONV_HALO, d), F32)],
        scratch_shapes=[pltpu.VMEM((CONV_HALO + tm + CONV_TAIL, d), F32), pltpu.VMEM((tm, d), F32)],
        compiler_params=_cparams(("arbitrary",)),
        name="mixer_prompt",
    )(x, mods, mods, mods, g.reshape(1, d), w1, b1.reshape(1, -1), wdw, bdw.reshape(1, d),
      lng.reshape(1, d), lnb.reshape(1, d), w2, b2.reshape(1, d))


def _mixer_sample_body(x_ref, sh_ref, sc_ref, gt_ref, g_ref, w1_ref, b1_ref, st_ref, wdw_ref, bdw_ref,
                       lng_ref, lnb_ref, w2_ref, b2_ref, o_ref, u_ref, *, kw):
    d = x_ref.shape[-1]
    x = x_ref[...]
    h = _modnorm(x, g_ref[...], sh_ref[...], sc_ref[...])
    a = _bdot(h.astype(BF16), w1_ref[...]) + b1_ref[...]
    u = a[:, :d] * jax.nn.sigmoid(a[:, d:])
    u_ref[...] = u
    y = bdw_ref[...] + wdw_ref[kw - 1:kw, :] * u
    for j in range(kw - 1):
        y = y + wdw_ref[j:j + 1, :] * st_ref[j]
    z = _layer_norm_silu(y, lng_ref[...], lnb_ref[...])
    out = _bdot(z.astype(BF16), w2_ref[...]) + b2_ref[...]
    o_ref[...] = x + gt_ref[...] * out


def _mixer_sample(x, mods, g, w1, b1, state_t, wdw, bdw, lng, lnb, w2, b2):
    bsz, d = x.shape
    kw = wdw.shape[0]
    tb = 32
    body = functools.partial(_mixer_sample_body, kw=kw)
    return pl.pallas_call(
        body,
        grid=(bsz // tb,),
        in_specs=[pl.BlockSpec((tb, d), lambda i: (i, 0)),
                  _mod_spec(mods, 0, d, tb), _mod_spec(mods, 1, d, tb), _mod_spec(mods, 2, d, tb),
                  _const_spec((1, d)), _const_spec(w1.shape), _const_spec((1, 2 * d)),
                  pl.BlockSpec((kw - 1, tb, d), lambda i: (0, i, 0)),
                  _const_spec(wdw.shape), _const_spec((1, d)), _const_spec((1, d)), _const_spec((1, d)),
                  _const_spec(w2.shape), _const_spec((1, d))],
        out_specs=[pl.BlockSpec((tb, d), lambda i: (i, 0)), pl.BlockSpec((tb, d), lambda i: (i, 0))],
        out_shape=[jax.ShapeDtypeStruct((bsz, d), F32), jax.ShapeDtypeStruct((bsz, d), F32)],
        compiler_params=_cparams(("parallel",)),
        name="mixer_sample",
    )(x, mods, mods, mods, g.reshape(1, d), w1, b1.reshape(1, -1), state_t, wdw, bdw.reshape(1, d),
      lng.reshape(1, d), lnb.reshape(1, d), w2, b2.reshape(1, d))


def _route_top2(logits, n_e):
    lane = lax.broadcasted_iota(jnp.int32, logits.shape, 1)
    m1 = jnp.max(logits, -1, keepdims=True)
    i1 = jnp.min(jnp.where(logits == m1, lane, LANE), -1, keepdims=True)
    sel1 = lane == i1
    rest = jnp.where(sel1, -jnp.inf, logits)
    m2 = jnp.max(rest, -1, keepdims=True)
    i2 = jnp.min(jnp.where(rest == m2, lane, LANE), -1, keepdims=True)
    sel2 = lane == i2
    e2 = jnp.exp(m2 - m1)
    den = 1.0 + e2
    return jnp.where(sel1, 1.0 / den, 0.0) + jnp.where(sel2, e2 / den, 0.0)


def _ffn_body(*refs, moe, pre, final, f_chunks):
    refs = list(refs)
    x_ref = refs.pop(0)
    if pre:
        ao_ref, wo_ref, ga_ref = refs.pop(0), refs.pop(0), refs.pop(0)
    sh_ref, sc_ref, gt_ref, g_ref = refs.pop(0), refs.pop(0), refs.pop(0), refs.pop(0)
    if moe:
        wr_ref, br_ref = refs.pop(0), refs.pop(0)
    wg_ref, wu_ref, wd_ref = refs.pop(0), refs.pop(0), refs.pop(0)
    if final:
        fg_ref = refs.pop(0)
    o_ref = refs.pop(0)
    h_scr = refs.pop(0)
    if pre:
        xs_scr = refs.pop(0)
    if moe:
        gates_scr = refs.pop(0)

    if moe:
        e, f = pl.program_id(1), pl.program_id(2)
        first = jnp.logical_and(e == 0, f == 0)
        last = jnp.logical_and(e == pl.num_programs(1) - 1, f == pl.num_programs(2) - 1)
    else:
        f = pl.program_id(1)
        first = f == 0
        last = f == pl.num_programs(1) - 1

    @pl.when(first)
    def _():
        x = x_ref[...]
        if pre:
            x = x + ga_ref[...] * _bdot(ao_ref[...], wo_ref[...])
            xs_scr[...] = x
        h = _modnorm(x, g_ref[...], sh_ref[...], sc_ref[...])
        h_scr[...] = h.astype(BF16)
        if moe:
            logits = jnp.dot(h, wr_ref[...], precision=lax.Precision.HIGHEST,
                             preferred_element_type=F32) + br_ref[...]
            gates_scr[...] = _route_top2(logits, None)
        o_ref[...] = jnp.zeros(o_ref.shape, F32)

    hb = h_scr[...]
    acc = None
    for c0, cw in f_chunks:
        gg = _bdot(hb, wg_ref[:, c0:c0 + cw])
        uu = _bdot(hb, wu_ref[:, c0:c0 + cw])
        part = _bdot((_silu(gg) * uu).astype(BF16), wd_ref[c0:c0 + cw, :])
        acc = part if acc is None else acc + part
    if moe:
        gates = gates_scr[...]
        lane = lax.broadcasted_iota(jnp.int32, gates.shape, 1)
        acc = acc * jnp.sum(jnp.where(lane == e, gates, 0.0), -1, keepdims=True)
    o_ref[...] += acc

    @pl.when(last)
    def _():
        xb = xs_scr[...] if pre else x_ref[...]
        r = xb + gt_ref[...] * o_ref[...]
        if final:
            r = _rms(r, fg_ref[...])
        o_ref[...] = r


def _ffn(x, mods, g, wg, wu, wd, *, tm, attn=None, router=None, final_g=None):
    m, d = x.shape
    moe = router is not None
    pre = attn is not None
    final = final_g is not None
    nf_total = wg.shape[-1]
    tf = nf_total // 2
    assert tf % LANE == 0 and m % tm == 0
    f_chunks = tuple((c0, min(512, tf - c0)) for c0 in range(0, tf, 512))
    if moe:
        n_e = wg.shape[0]
        grid = (m // tm, n_e, nf_total // tf)
        wgu_spec = pl.BlockSpec((None, d, tf), lambda i, e, f: (e, 0, f))
        wd_spec = pl.BlockSpec((None, tf, d), lambda i, e, f: (e, f, 0))
        sem = ("parallel", "arbitrary", "arbitrary")
    else:
        grid = (m // tm, nf_total // tf)
        wgu_spec = pl.BlockSpec((d, tf), lambda i, f: (0, f))
        wd_spec = pl.BlockSpec((tf, d), lambda i, f: (f, 0))
        sem = ("parallel", "arbitrary")
    row_spec = pl.BlockSpec((tm, d), lambda i, *_: (i, 0))

    args, specs = [x], [row_spec]
    if pre:
        ao, wo, amods = attn
        args += [ao, wo, amods]
        specs += [row_spec, _const_spec(wo.shape), _mod_spec(amods, 2, d, tm)]
    args += [mods, mods, mods, g.reshape(1, d)]
    specs += [_mod_spec(mods, 3, d, tm), _mod_spec(mods, 4, d, tm), _mod_spec(mods, 5, d, tm),
              _const_spec((1, d))]
    if moe:
        args += list(router)
        specs += [_const_spec(router[0].shape), _const_spec(router[1].shape)]
    args += [wg, wu, wd]
    specs += [wgu_spec, wgu_spec, wd_spec]
    if final:
        args.append(final_g.reshape(1, d))
        specs.append(_const_spec((1, d)))
    scratch = [pltpu.VMEM((tm, d), BF16)]
    if pre:
        scratch.append(pltpu.VMEM((tm, d), F32))
    if moe:
        scratch.append(pltpu.VMEM((tm, LANE), F32))
    body = functools.partial(_ffn_body, moe=moe, pre=pre, final=final, f_chunks=f_chunks)
    return pl.pallas_call(
        body, grid=grid, in_specs=specs, out_specs=row_spec,
        out_shape=jax.ShapeDtypeStruct((m, d), F32),
        scratch_shapes=scratch, compiler_params=_cparams(sem),
        name="ffn_moe" if moe else "ffn_dense",
    )(*args)


def _kvq_body(*refs, emit_kv, n_heads):
    (x_ref, ksh_ref, ksc_ref, kg_ref, wdkv_ref, kvg_ref, wkr_ref, wkrr_ref, cos32_ref, sin32_ref,
     qsh_ref, qsc_ref, qg_ref, wdq_ref, qng_ref, wq_ref, wqr_ref, cosq_ref, sinq_ref) = refs[:19]
    refs = refs[19:]
    if emit_kv:
        wk_ref, ek_ref, wuvt_ref, vone_ref = refs[:4]
        refs = refs[4:]
    ckv_o, kpe_o, q_o = refs[:3]
    x = x_ref[...]
    hk = _modnorm(x, kg_ref[...], ksh_ref[...], ksc_ref[...]).astype(BF16)
    ckv = _rms(_bdot(hk, wdkv_ref[...]), kvg_ref[...])
    ckv_o[...] = ckv
    kpe = _bdot(hk, wkr_ref[...]) * cos32_ref[...] + _bdot(hk, wkrr_ref[...]) * sin32_ref[...]
    kpe_o[...] = kpe

    hq = _modnorm(x, qg_ref[...], qsh_ref[...], qsc_ref[...]).astype(BF16)
    cq = _rms(_bdot(hq, wdq_ref[...]), qng_ref[...]).astype(BF16)
    q = _bdot(cq, wq_ref[...])
    qr = _bdot(cq, wqr_ref[...])
    cosq, sinq = cosq_ref[...], sinq_ref[...]
    for h in range(n_heads):
        sl = slice(h * HEAD_PAD, (h + 1) * HEAD_PAD)
        q_o[:, sl] = (q[:, sl] * cosq + qr[:, sl] * sinq).astype(BF16)
    if emit_kv:
        k_o, vt_o = refs[3:5]
        cb = ckv.astype(BF16)
        k_o[...] = (_bdot(cb, wk_ref[...]) + _bdot(kpe.astype(BF16), ek_ref[...])).astype(BF16)
        vt = lax.dot_general(wuvt_ref[...], cb, (((1,), (1,)), ((), ())), preferred_element_type=F32)
        vt_o[...] = (vt + vone_ref[...]).astype(BF16)


def _kvq(x, kv_mods, q_mods, w, tabs, *, tm, emit_kv):
    m, d = x.shape
    n_heads = w["wq"].shape[1] // HEAD_PAD
    kvl, rope = w["wdkv"].shape[1], w["wkr"].shape[1]
    row = lambda n: pl.BlockSpec((tm, n), lambda i: (i, 0))
    tab = lambda t: (pl.BlockSpec((1, t.shape[1]), lambda i: (0, 0)) if t.shape[0] == 1
                     else pl.BlockSpec((tm, t.shape[1]), lambda i: (i, 0)))
    cos32, sin32, cosq, sinq = tabs
    args = [x, kv_mods, kv_mods, w["kg"], w["wdkv"], w["kvg"], w["wkr"], w["wkrr"], cos32, sin32,
            q_mods, q_mods, w["qg"], w["wdq"], w["qng"], w["wq"], w["wqr"], cosq, sinq]
    specs = [row(d), _mod_spec(kv_mods, 0, d, tm), _mod_spec(kv_mods, 1, d, tm), _const_spec((1, d)),
             _const_spec(w["wdkv"].shape), _const_spec((1, kvl)), _const_spec(w["wkr"].shape),
             _const_spec(w["wkrr"].shape), tab(cos32), tab(sin32),
             _mod_spec(q_mods, 0, d, tm), _mod_spec(q_mods, 1, d, tm), _const_spec((1, d)),
             _const_spec(w["wdq"].shape), _const_spec(w["qng"].shape), _const_spec(w["wq"].shape),
             _const_spec(w["wqr"].shape), tab(cosq), tab(sinq)]
    out_shape = [jax.ShapeDtypeStruct((m, kvl), F32), jax.ShapeDtypeStruct((m, rope), F32),
                 jax.ShapeDtypeStruct((m, n_heads * HEAD_PAD), BF16)]
    out_specs = [row(kvl), row(rope), row(n_heads * HEAD_PAD)]
    if emit_kv:
        vrows = w["wuvt"].shape[0]
        args += [w["wk"], w["ek"], w["wuvt"], w["vone"]]
        specs += [_const_spec(w["wk"].shape), _const_spec(w["ek"].shape), _const_spec(w["wuvt"].shape),
                  _const_spec(w["vone"].shape)]
        out_shape += [jax.ShapeDtypeStruct((m, n_heads * HEAD_PAD), BF16),
                      jax.ShapeDtypeStruct((vrows, m), BF16)]
        out_specs += [row(n_heads * HEAD_PAD), pl.BlockSpec((vrows, tm), lambda i: (0, i))]
    body = functools.partial(_kvq_body, emit_kv=emit_kv, n_heads=n_heads)
    return pl.pallas_call(
        body, grid=(m // tm,), in_specs=specs, out_specs=out_specs, out_shape=out_shape,
        compiler_params=_cparams(("parallel",)), name="kvq",
    )(*args)


V_ROWS = 80


def _flash_body(q_ref, k_ref, vt_ref, o_ref, s_scr, p_scr, m_scr, acc_scr, *, tq, v_head):
    qi = pl.program_id(1)
    heads = (0, 1)
    for hh in heads:
        m_scr[hh] = jnp.full((1, tq), NEG, F32)
        acc_scr[hh] = jnp.zeros((V_ROWS, tq), F32)

    def qk(j, buf):
        k0 = pl.multiple_of(j * tq, tq)
        for hh in heads:
            q = q_ref[:, hh * HEAD_PAD:(hh + 1) * HEAD_PAD]
            k = k_ref[pl.ds(k0, tq), hh * HEAD_PAD:(hh + 1) * HEAD_PAD]
            s_scr[buf, hh] = lax.dot_general(k, q, (((1,), (1,)), ((), ())), preferred_element_type=F32)

    def softmax_pv(j, buf, masked):
        k0 = pl.multiple_of(j * tq, tq)
        alphas = []
        for hh in heads:
            if masked:
                krow = lax.broadcasted_iota(jnp.int32, (tq, tq), 0)
                qcol = lax.broadcasted_iota(jnp.int32, (tq, tq), 1)
                s_scr[buf, hh] = jnp.where(krow <= qcol, s_scr[buf, hh], NEG)
            m_old = m_scr[hh]
            m_new = jnp.maximum(m_old, jnp.max(s_scr[buf, hh], 0, keepdims=True))
            p_scr[hh] = jnp.exp2(s_scr[buf, hh] - m_new).astype(BF16)
            m_scr[hh] = m_new
            alphas.append(jnp.exp2(m_old - m_new))
        for hh in heads:
            vt = vt_ref[hh * V_ROWS:(hh + 1) * V_ROWS, pl.ds(k0, tq)]
            acc_scr[hh] = alphas[hh] * acc_scr[hh] + _bdot(vt, p_scr[hh])

    qk(qi, 0)
    softmax_pv(qi, 0, True)

    @pl.when(qi > 0)
    def _():
        qk(0, 0)

    def pair(i, carry):
        j = 2 * i
        qk(j + 1, 1)
        softmax_pv(j, 0, False)
        qk(j + 2, 0)
        softmax_pv(j + 1, 1, False)
        return carry

    lax.fori_loop(0, qi // 2, pair, 0)

    @pl.when(qi % 2 == 1)
    def _():
        softmax_pv(qi - 1, 0, False)

    outs = [acc_scr[hh, :v_head] / acc_scr[hh, v_head:v_head + 1] for hh in heads]
    o_ref[...] = jnp.concatenate(outs, 0).T.astype(BF16)


def _flash_prompt(q_all, k_all, vt_all, *, v_head):
    s = q_all.shape[0]
    n_pairs = q_all.shape[1] // (2 * HEAD_PAD)
    assert vt_all.shape[0] == n_pairs * 2 * V_ROWS and 2 * v_head == LANE and v_head < V_ROWS
    tq = 512
    body = functools.partial(_flash_body, tq=tq, v_head=v_head)
    return pl.pallas_call(
        body,
        grid=(n_pairs, s // tq),
        in_specs=[pl.BlockSpec((tq, 2 * HEAD_PAD), lambda hp, qi: (qi, hp)),
                  pl.BlockSpec((s, 2 * HEAD_PAD), lambda hp, qi: (0, hp)),
                  pl.BlockSpec((2 * V_ROWS, s), lambda hp, qi: (hp, 0))],
        out_specs=pl.BlockSpec((tq, 2 * v_head), lambda hp, qi: (qi, hp)),
        out_shape=jax.ShapeDtypeStruct((s, n_pairs * 2 * v_head), BF16),
        scratch_shapes=[pltpu.VMEM((2, 2, tq, tq), F32), pltpu.VMEM((2, tq, tq), BF16),
                        pltpu.VMEM((2, 1, tq), F32), pltpu.VMEM((2, V_ROWS, tq), F32)],
        compiler_params=_cparams(("parallel", "arbitrary")),
        name="flash_prompt",
    )(q_all, k_all, vt_all)


def _qlat_body(q_ref, w_ref, o_ref, *, n_heads, kvl):
    for h in range(n_heads):
        o_ref[:, h * kvl:(h + 1) * kvl] = _bdot(q_ref[:, h * HEAD_PAD:(h + 1) * HEAD_PAD], w_ref[h]).astype(BF16)


def _qlat(q_all, wukt):
    b = q_all.shape[0]
    n_heads, _, kvl = wukt.shape
    body = functools.partial(_qlat_body, n_heads=n_heads, kvl=kvl)
    return pl.pallas_call(
        body, grid=(1,),
        in_specs=[_const_spec(q_all.shape), _const_spec(wukt.shape)],
        out_specs=_const_spec((b, n_heads * kvl)),
        out_shape=jax.ShapeDtypeStruct((b, n_heads * kvl), BF16),
        compiler_params=_cparams(("arbitrary",)), name="q_latent",
    )(q_all, wukt)


def _ouv_body(o_ref, w_ref, out_ref, *, n_heads, kvl):
    for hp in range(n_heads // 2):
        acc = None
        for h in (2 * hp, 2 * hp + 1):
            t = _bdot(o_ref[:, h * kvl:(h + 1) * kvl].astype(BF16), w_ref[h])
            acc = t if acc is None else acc + t
        out_ref[:, hp * LANE:(hp + 1) * LANE] = acc.astype(BF16)


def _ouv(o_lat, wuvp):
    b = o_lat.shape[0]
    n_heads, kvl, _ = wuvp.shape
    body = functools.partial(_ouv_body, n_heads=n_heads, kvl=kvl)
    return pl.pallas_call(
        body, grid=(1,),
        in_specs=[_const_spec(o_lat.shape), _const_spec(wuvp.shape)],
        out_specs=_const_spec((b, n_heads // 2 * LANE)),
        out_shape=jax.ShapeDtypeStruct((b, n_heads // 2 * LANE), BF16),
        compiler_params=_cparams(("arbitrary",)), name="o_up",
    )(o_lat, wuvp)


PAGES_PER_CHUNK = 16


KEY_SPLIT = 4


def _paged_body(pt_ref, ql_ref, qp_ref, cn_ref, kn_ref, cc_hbm, kt_hbm, o_ref, cbuf, kbuf, sem,
                *, n_pages, page):
    b = pl.program_id(0)
    nb = pl.num_programs(0)
    ch = PAGES_PER_CHUNK
    nch = n_pages // ch

    def copies(bb, c, slot):
        out = []
        for p in range(ch):
            pg = pt_ref[bb, c * ch + p]
            dst = pl.ds(p * page, page)
            out.append(pltpu.make_async_copy(cc_hbm.at[pg], cbuf.at[slot, dst], sem.at[0, slot]))
            out.append(pltpu.make_async_copy(kt_hbm.at[pg], kbuf.at[slot, :, dst], sem.at[1, slot]))
        return out

    def start(bb, c, slot):
        for cp in copies(bb, c, slot):
            cp.start()

    def wait(bb, c, slot):
        for cp in copies(bb, c, slot):
            cp.wait()

    @pl.when(b == 0)
    def _():
        start(0, 0, 0)

    ql = ql_ref[...]
    qp = qp_ref[...]
    dn = (((1,), (1,)), ((), ()))

    sub = ch * page // KEY_SPLIT

    def attend(slot, carry):
        cks = [cbuf[slot, i * sub:(i + 1) * sub, :].astype(BF16) for i in range(KEY_SPLIT)]
        ss = [lax.dot_general(ql, cks[i], dn, preferred_element_type=F32)
              + _bdot(qp, kbuf[slot, :, i * sub:(i + 1) * sub].astype(BF16)) for i in range(KEY_SPLIT)]
        ms = [jnp.maximum(carry[i][0], jnp.max(ss[i], -1, keepdims=True)) for i in range(KEY_SPLIT)]
        ps = [jnp.exp2(ss[i] - ms[i]) for i in range(KEY_SPLIT)]
        pvs = [_bdot(ps[i].astype(BF16), cks[i]) for i in range(KEY_SPLIT)]
        new = []
        for i in range(KEY_SPLIT):
            m, l, acc = carry[i]
            alpha = jnp.exp2(m - ms[i])
            new.append((ms[i], alpha * l + jnp.sum(ps[i], -1, keepdims=True), alpha * acc + pvs[i]))
        return tuple(new)

    def pair(c2, carry):
        c = 2 * c2
        wait(b, c, 0)
        start(b, c + 1, 1)
        carry = attend(0, carry)
        wait(b, c + 1, 1)

        @pl.when(c + 2 < nch)
        def _():
            start(b, c + 2, 0)

        @pl.when(jnp.logical_and(c + 2 == nch, b + 1 < nb))
        def _():
            start(b + 1, 0, 0)

        return attend(1, carry)

    nh = ql.shape[0]
    one = (jnp.full((nh, 1), NEG, F32), jnp.zeros((nh, 1), F32), jnp.zeros((nh, ql.shape[1]), F32))
    states = lax.fori_loop(0, nch // 2, pair, (one,) * KEY_SPLIT)

    cn = cn_ref[...]
    s_new = (jnp.sum(ql.astype(F32) * cn, -1, keepdims=True)
             + jnp.sum(qp.astype(F32) * kn_ref[...], -1, keepdims=True))
    m_f = s_new
    for m, _, _ in states:
        m_f = jnp.maximum(m_f, m)
    w_new = jnp.exp2(s_new - m_f)
    l_f = w_new
    acc_f = w_new * cn
    for m, l, acc in states:
        w = jnp.exp2(m - m_f)
        l_f = l_f + w * l
        acc_f = acc_f + w * acc
    o_ref[...] = acc_f / l_f


def _paged_attention(page_table, q_lat, q_pe, ckv_new, kpe_new, cache_ckv, cache_kpe_t):
    bsz, nh, kvl = q_lat.shape
    rope = q_pe.shape[-1]
    n_pages = page_table.shape[1]
    page = cache_ckv.shape[1]
    assert n_pages % (2 * PAGES_PER_CHUNK) == 0 and (PAGES_PER_CHUNK * page) % (KEY_SPLIT * LANE) == 0
    rows = PAGES_PER_CHUNK * page
    blk = lambda a, b_: pl.BlockSpec((None, a, b_), lambda b, pt: (b, 0, 0))
    body = functools.partial(_paged_body, n_pages=n_pages, page=page)
    return pl.pallas_call(
        body,
        grid_spec=pltpu.PrefetchScalarGridSpec(
            num_scalar_prefetch=1, grid=(bsz,),
            in_specs=[blk(nh, kvl), blk(nh, rope), blk(1, kvl), blk(1, rope),
                      pl.BlockSpec(memory_space=pl.ANY), pl.BlockSpec(memory_space=pl.ANY)],
            out_specs=blk(nh, kvl),
            scratch_shapes=[pltpu.VMEM((2, rows, kvl), F32), pltpu.VMEM((2, rope, rows), F32),
                            pltpu.SemaphoreType.DMA((2, 2))]),
        out_shape=jax.ShapeDtypeStruct((bsz, nh, kvl), F32),
        compiler_params=_cparams(("arbitrary",)),
        name="paged_attention",
    )(page_table, q_lat, q_pe, ckv_new, kpe_new, cache_ckv, cache_kpe_t)


def _rot_half_cols(w):
    half = w.shape[-1] // 2
    return jnp.concatenate([-w[..., half:], w[..., :half]], -1)


def _rope_tables(pos, rope, nope, scale):
    half = rope // 2
    inv = ROPE_THETA ** (-jnp.arange(half, dtype=F32) / half)
    ang = pos.astype(F32)[:, None] * inv[None, :]
    cos, sin = jnp.cos(ang), jnp.sin(ang)
    cos32 = jnp.concatenate([cos, cos], -1)
    sin32 = jnp.concatenate([sin, sin], -1)
    n = pos.shape[0]
    pad = HEAD_PAD - nope - rope
    cosq = jnp.concatenate([jnp.ones((n, nope), F32), cos32, jnp.zeros((n, pad), F32)], -1) * scale
    sinq = jnp.concatenate([jnp.zeros((n, nope), F32), sin32, jnp.zeros((n, pad), F32)], -1) * scale
    return cos32, sin32, cosq, sinq


def kernel(x_prompt, x_sample, c_prompt, c_sample, state_conv, cache_ckv, cache_kpe, page_table, ada_w, ada_b, norm_mix_g, norm_ffn_g, conv_w1, conv_b1, conv_wdw, conv_bdw, conv_ln_g, conv_ln_b, conv_w2, conv_b2, ada_kv_w, ada_kv_b, norm_kv_g, w_dkv, kv_norm_g, w_kr, w_uk, w_uv, w_dq, q_norm_g, w_uq, w_o, ffn_w_gate, ffn_w_up, ffn_w_down, router_w, router_b, moe_w_gate, moe_w_up, moe_w_down, final_norm_g):
    _, s, d = x_prompt.shape
    bd = x_sample.shape[0]
    assert x_prompt.shape[0] == 1 and x_sample.shape[1] == 1 and ada_w.shape[0] == 2
    kvl, n_heads, nope = w_uk.shape
    v_head = w_uv.shape[2]
    rope = w_kr.shape[1]
    q_lora = w_dq.shape[2]
    n_e = router_w.shape[2]
    page = cache_ckv.shape[1]
    past = page_table.shape[1] * page
    pad = HEAD_PAD - nope - rope
    sm_scale = float(nope + rope) ** -0.5 * LOG2_E

    w1 = conv_w1[0].astype(BF16)
    w2 = conv_w2[0].astype(BF16)
    wg0, wu0, wd0 = ffn_w_gate[0].astype(BF16), ffn_w_up[0].astype(BF16), ffn_w_down[0].astype(BF16)
    wge, wue, wde = moe_w_gate[0].astype(BF16), moe_w_up[0].astype(BF16), moe_w_down[0].astype(BF16)
    wo = w_o[0].astype(BF16)
    wr = jnp.pad(router_w[0], ((0, 0), (0, LANE - n_e)))
    br = jnp.pad(router_b[0], (0, LANE - n_e), constant_values=-jnp.inf).reshape(1, LANE)
    wuq = w_uq[0]
    wuq_rot = jnp.concatenate([jnp.zeros_like(wuq[..., :nope]), _rot_half_cols(wuq[..., nope:])], -1)
    pad_q = lambda w: jnp.pad(w, ((0, 0), (0, 0), (0, pad))).reshape(q_lora, n_heads * HEAD_PAD).astype(BF16)
    wk = jnp.pad(w_uk, ((0, 0), (0, 0), (0, HEAD_PAD - nope))).reshape(kvl, n_heads * HEAD_PAD).astype(BF16)
    ek = jnp.tile(jnp.pad(jnp.eye(rope, dtype=F32), ((0, 0), (nope, pad))), (1, n_heads)).astype(BF16)
    wuvt = jnp.pad(jnp.transpose(w_uv, (1, 2, 0)), ((0, 0), (0, V_ROWS - v_head), (0, 0)))
    wuvt = wuvt.reshape(n_heads * V_ROWS, kvl).astype(BF16)
    vone = jnp.tile((jnp.arange(V_ROWS) == v_head).astype(F32), n_heads).reshape(n_heads * V_ROWS, 1)
    kvq_w = dict(
        kg=norm_kv_g.reshape(1, d), wdkv=w_dkv.astype(BF16), kvg=kv_norm_g.reshape(1, kvl),
        wkr=w_kr.astype(BF16), wkrr=_rot_half_cols(w_kr).astype(BF16),
        qg=norm_mix_g[1].reshape(1, d), wdq=w_dq[0].astype(BF16), qng=q_norm_g[0].reshape(1, q_lora),
        wq=pad_q(wuq), wqr=pad_q(wuq_rot), wk=wk, ek=ek, wuvt=wuvt, vone=vone)
    wukt = jnp.pad(jnp.transpose(w_uk, (1, 2, 0)), ((0, 0), (0, HEAD_PAD - nope), (0, 0))).astype(BF16)
    wuv_h = jnp.transpose(w_uv, (1, 0, 2))
    wuvp = jnp.where((jnp.arange(n_heads) % 2 == 0)[:, None, None],
                     jnp.pad(wuv_h, ((0, 0), (0, 0), (0, v_head))),
                     jnp.pad(wuv_h, ((0, 0), (0, 0), (v_head, 0)))).astype(BF16)

    n_c = bd + 1
    c_all = jnp.pad(jnp.concatenate([c_sample, c_prompt], 0), ((0, -n_c % SUBLANE), (0, 0)))
    mods0 = _ada(c_all, ada_w, ada_b[0], layer=0)
    mods1 = _ada(c_all, ada_w, ada_b[1], layer=1)
    modskv = _ada(c_all, ada_kv_w, ada_kv_b)
    mp0, mp1, mpkv = mods0[bd:n_c], mods1[bd:n_c], modskv[bd:n_c]
    ms0, ms1, mskv = mods0[:bd], mods1[:bd], modskv[:bd]

    xp = x_prompt[0]
    x1, st_p = _mixer_prompt(xp, mp0, norm_mix_g[0], w1, conv_b1[0], conv_wdw[0], conv_bdw[0],
                             conv_ln_g[0], conv_ln_b[0], w2, conv_b2[0])
    x2 = _ffn(x1, mp0, norm_ffn_g[0], wg0, wu0, wd0, tm=512)
    tabs_p = _rope_tables(jnp.arange(s, dtype=jnp.int32), rope, nope, sm_scale)
    ckv_p, kpe_p, q_p, k_p, vt_p = _kvq(x2, mpkv, mp1, kvq_w, tabs_p, tm=512, emit_kv=True)
    o_p = _flash_prompt(q_p, k_p, vt_p, v_head=v_head)
    y_p = _ffn(x2, mp1, norm_ffn_g[1], wge, wue, wde, tm=512, attn=(o_p, wo, mp1),
               router=(wr, br), final_g=final_norm_g)

    xs = x_sample[:, 0]
    kw = conv_wdw.shape[1]
    state_t = jnp.transpose(state_conv[0], (1, 0, 2))
    xs1, u_s = _mixer_sample(xs, ms0, norm_mix_g[0], w1, conv_b1[0], state_t, conv_wdw[0], conv_bdw[0],
                             conv_ln_g[0], conv_ln_b[0], w2, conv_b2[0])
    xs2 = _ffn(xs1, ms0, norm_ffn_g[0], wg0, wu0, wd0, tm=bd)
    tabs_s = _rope_tables(jnp.full((1,), past, jnp.int32), rope, nope, sm_scale)
    ckv_s, kpe_s, q_s = _kvq(xs2, mskv, ms1, kvq_w, tabs_s, tm=bd, emit_kv=False)
    q_lat = _qlat(q_s, wukt).reshape(bd, n_heads, kvl)
    q_pe = q_s.reshape(bd, n_heads, HEAD_PAD)[:, :, nope:nope + rope]
    o_lat = _paged_attention(page_table, q_lat, q_pe, ckv_s.reshape(bd, 1, kvl), kpe_s.reshape(bd, 1, rope),
                             cache_ckv, jnp.transpose(cache_kpe, (0, 2, 1)))
    o_s = _ouv(o_lat.reshape(bd, n_heads * kvl), wuvp)
    y_s = _ffn(xs2, ms1, norm_ffn_g[1], wge, wue, wde, tm=bd, attn=(o_s, wo, ms1),
               router=(wr, br), final_g=final_norm_g)

    conv_state_prompt = st_p[CONV_HALO - (kw - 1):][None, None]
    conv_state_sample = jnp.concatenate([state_conv[0][:, 1:], u_s[:, None]], 1)[None]
    return (y_p[None], y_s[:, None], conv_state_prompt, conv_state_sample,
            ckv_p.reshape(1, s // page, page, kvl), kpe_p.reshape(1, s // page, page, rope),
            ckv_s[:, None], kpe_s[:, None])
```

```python
import functools

import jax
import jax.numpy as jnp
from jax import lax
from jax.experimental import pallas as pl
from jax.experimental.pallas import tpu as pltpu

F32 = jnp.float32
BF16 = jnp.bfloat16

NORM_EPS = 1e-6
ROPE_THETA = 10000.0
NEG = -1e30
N_MOD = 6
LOG2_E = 1.4426950408889634

V7X_VMEM_BYTES = 64 * 1024 * 1024
VMEM_LIMIT = 56 * 1024 * 1024
LANE = 128
SUBLANE = 8
HEAD_PAD = 128


def _cparams(sem):
    return pltpu.CompilerParams(dimension_semantics=sem, vmem_limit_bytes=VMEM_LIMIT)


def _rms(x, g):
    return x * lax.rsqrt(jnp.mean(x * x, -1, keepdims=True) + NORM_EPS) * g


def _modnorm(x, g, shift, scale):
    return _rms(x, g) * (1.0 + scale) + shift


def _silu(x):
    return x * jax.nn.sigmoid(x)


def _bdot(a, b):
    return jnp.dot(a, b, preferred_element_type=F32)


def _ada_body(c_ref, w_ref, b_ref, o_ref):
    s = _silu(c_ref[...]).astype(BF16)
    o_ref[...] = _bdot(s, w_ref[...].astype(BF16)) + b_ref[...]


def _ada(c_all, w, b, layer=None):
    m, d = c_all.shape
    n = w.shape[-1]
    tn = 1024
    if layer is None:
        w_spec = pl.BlockSpec((d, tn), lambda j: (0, j))
    else:
        w_spec = pl.BlockSpec((None, d, tn), lambda j: (layer, 0, j))
    return pl.pallas_call(
        _ada_body,
        grid=(n // tn,),
        in_specs=[pl.BlockSpec((m, d), lambda j: (0, 0)), w_spec,
                  pl.BlockSpec((1, tn), lambda j: (0, j))],
        out_specs=pl.BlockSpec((m, tn), lambda j: (0, j)),
        out_shape=jax.ShapeDtypeStruct((m, n), F32),
        compiler_params=_cparams(("parallel",)),
        name="ada_mod",
    )(c_all, w, b.reshape(1, n))


def _mod_spec(mods, k, d, tm):
    if mods.shape[0] == 1:
        return pl.BlockSpec((1, d), lambda i, *_: (0, k))
    return pl.BlockSpec((tm, d), lambda i, *_: (i, k))


def _const_spec(shape):
    nd = len(shape)
    return pl.BlockSpec(shape, lambda *_: (0,) * nd)


CONV_HALO = 32
CONV_TAIL = 16
CONV_RC = 32
CONV_CC = 256


def _layer_norm_silu(y, g, b):
    mu = jnp.mean(y, -1, keepdims=True)
    yc = y - mu
    var = jnp.mean(yc * yc, -1, keepdims=True)
    return _silu(yc * lax.rsqrt(var + NORM_EPS) * g + b)


def _mixer_prompt_body(x_ref, sh_ref, sc_ref, gt_ref, g_ref, w1_ref, b1_ref, wdw_ref, bdw_ref,
                       lng_ref, lnb_ref, w2_ref, b2_ref, o_ref, st_ref, ubuf, ybuf, *, tm, kw):
    d = x_ref.shape[-1]
    i = pl.program_id(0)

    @pl.when(i == 0)
    def _():
        ubuf[...] = jnp.zeros(ubuf.shape, F32)

    x = x_ref[...]
    h = _modnorm(x, g_ref[...], sh_ref[...], sc_ref[...])
    a = _bdot(h.astype(BF16), w1_ref[...]) + b1_ref[...]
    ubuf[CONV_HALO:CONV_HALO + tm, :] = a[:, :d] * jax.nn.sigmoid(a[:, d:])

    off = CONV_HALO - (kw - 1)
    nq = (kw - 1 + off) // SUBLANE + 1

    def chunk(rc, carry):
        r0 = pl.multiple_of(rc * CONV_RC, CONV_RC)
        for c0 in range(0, d, CONV_CC):
            win = ubuf[pl.ds(r0, CONV_RC + SUBLANE * (nq + 1)), c0:c0 + CONV_CC]
            y = None
            for r in range(SUBLANE):
                a_r = None
                for q in range(nq):
                    j = SUBLANE * q + r - off
                    if 0 <= j < kw:
                        t = wdw_ref[j:j + 1, c0:c0 + CONV_CC] * win[SUBLANE * q:SUBLANE * q + CONV_RC + SUBLANE]
                        a_r = t if a_r is None else a_r + t
                if a_r is not None:
                    s = a_r[r:r + CONV_RC]
                    y = s if y is None else y + s
            ybuf[pl.ds(r0, CONV_RC), c0:c0 + CONV_CC] = y + bdw_ref[:, c0:c0 + CONV_CC]
        return carry

    lax.fori_loop(0, tm // CONV_RC, chunk, 0)

    z = _layer_norm_silu(ybuf[...], lng_ref[...], lnb_ref[...])
    out = _bdot(z.astype(BF16), w2_ref[...]) + b2_ref[...]
    o_ref[...] = x + gt_ref[...] * out
    tail = ubuf[tm:tm + CONV_HALO, :]
    ubuf[0:CONV_HALO, :] = tail
    st_ref[...] = tail


def _mixer_prompt(x, mods, g, w1, b1, wdw, bdw, lng, lnb, w2, b2):
    s, d = x.shape
    kw = wdw.shape[0]
    tm = 512
    assert s % tm == 0 and kw - 1 <= CONV_HALO and s >= CONV_HALO
    body = functools.partial(_mixer_prompt_body, tm=tm, kw=kw)
    return pl.pallas_call(
        body,
        grid=(s // tm,),
        in_specs=[pl.BlockSpec((tm, d), lambda i: (i, 0)),
                  _mod_spec(mods, 0, d, tm), _mod_spec(mods, 1, d, tm), _mod_spec(mods, 2, d, tm),
                  _const_spec((1, d)), _const_spec(w1.shape), _const_spec((1, 2 * d)),
                  _const_spec(wdw.shape), _const_spec((1, d)), _const_spec((1, d)), _const_spec((1, d)),
                  _const_spec(w2.shape), _const_spec((1, d))],
        out_specs=[pl.BlockSpec((tm, d), lambda i: (i, 0)), _const_spec((CONV_HALO, d))],
        out_shape=[jax.ShapeDtypeStruct((s, d), F32), jax.ShapeDtypeStruct((CONV_HALO, d), F32)],
        scratch_shapes=[pltpu.VMEM((CONV_HALO + tm + CONV_TAIL, d), F32), pltpu.VMEM((tm, d), F32)],
        compiler_params=_cparams(("arbitrary",)),
        name="mixer_prompt",
    )(x, mods, mods, mods, g.reshape(1, d), w1, b1.reshape(1, -1), wdw, bdw.reshape(1, d),
      lng.reshape(1, d), lnb.reshape(1, d), w2, b2.reshape(1, d))


def _mixer_sample_body(x_ref, sh_ref, sc_ref, gt_ref, g_ref, w1_ref, b1_ref, st_ref, wdw_ref, bdw_ref,
                       lng_ref, lnb_ref, w2_ref, b2_ref, o_ref, u_ref, *, kw):
    d = x_ref.shape[-1]
    x = x_ref[...]
    h = _modnorm(x, g_ref[...], sh_ref[...], sc_ref[...])
    a = _bdot(h.astype(BF16), w1_ref[...]) + b1_ref[...]
    u = a[:, :d] * jax.nn.sigmoid(a[:, d:])
    u_ref[...] = u
    y = bdw_ref[...] + wdw_ref[kw - 1:kw, :] * u
    for j in range(kw - 1):
        y = y + wdw_ref[j:j + 1, :] * st_ref[j]
    z = _layer_norm_silu(y, lng_ref[...], lnb_ref[...])
    out = _bdot(z.astype(BF16), w2_ref[...]) + b2_ref[...]
    o_ref[...] = x + gt_ref[...] * out


def _mixer_sample(x, mods, g, w1, b1, state_t, wdw, bdw, lng, lnb, w2, b2):
    bsz, d = x.shape
    kw = wdw.shape[0]
    tb = 32
    body = functools.partial(_mixer_sample_body, kw=kw)
    return pl.pallas_call(
        body,
        grid=(bsz // tb,),
        in_specs=[pl.BlockSpec((tb, d), lambda i: (i, 0)),
                  _mod_spec(mods, 0, d, tb), _mod_spec(mods, 1, d, tb), _mod_spec(mods, 2, d, tb),
                  _const_spec((1, d)), _const_spec(w1.shape), _const_spec((1, 2 * d)),
                  pl.BlockSpec((kw - 1, tb, d), lambda i: (0, i, 0)),
                  _const_spec(wdw.shape), _const_spec((1, d)), _const_spec((1, d)), _const_spec((1, d)),
                  _const_spec(w2.shape), _const_spec((1, d))],
        out_specs=[pl.BlockSpec((tb, d), lambda i: (i, 0)), pl.BlockSpec((tb, d), lambda i: (i, 0))],
        out_shape=[jax.ShapeDtypeStruct((bsz, d), F32), jax.ShapeDtypeStruct((bsz, d), F32)],
        compiler_params=_cparams(("parallel",)),
        name="mixer_sample",
    )(x, mods, mods, mods, g.reshape(1, d), w1, b1.reshape(1, -1), state_t, wdw, bdw.reshape(1, d),
      lng.reshape(1, d), lnb.reshape(1, d), w2, b2.reshape(1, d))


def _route_top2(logits, n_e):
    lane = lax.broadcasted_iota(jnp.int32, logits.shape, 1)
    m1 = jnp.max(logits, -1, keepdims=True)
    i1 = jnp.min(jnp.where(logits == m1, lane, LANE), -1, keepdims=True)
    sel1 = lane == i1
    rest = jnp.where(sel1, -jnp.inf, logits)
    m2 = jnp.max(rest, -1, keepdims=True)
    i2 = jnp.min(jnp.where(rest == m2, lane, LANE), -1, keepdims=True)
    sel2 = lane == i2
    e2 = jnp.exp(m2 - m1)
    den = 1.0 + e2
    return jnp.where(sel1, 1.0 / den, 0.0) + jnp.where(sel2, e2 / den, 0.0)


def _ffn_body(*refs, moe, pre, final, f_chunks):
    refs = list(refs)
    x_ref = refs.pop(0)
    if pre:
        ao_ref, wo_ref, ga_ref = refs.pop(0), refs.pop(0), refs.pop(0)
    sh_ref, sc_ref, gt_ref, g_ref = refs.pop(0), refs.pop(0), refs.pop(0), refs.pop(0)
    if moe:
        wr_ref, br_ref = refs.pop(0), refs.pop(0)
    wg_ref, wu_ref, wd_ref = refs.pop(0), refs.pop(0), refs.pop(0)
    if final:
        fg_ref = refs.pop(0)
    o_ref = refs.pop(0)
    h_scr = refs.pop(0)
    if pre:
        xs_scr = refs.pop(0)
    if moe:
        gates_scr = refs.pop(0)

    if moe:
        e, f = pl.program_id(1), pl.program_id(2)
        first = jnp.logical_and(e == 0, f == 0)
        last = jnp.logical_and(e == pl.num_programs(1) - 1, f == pl.num_programs(2) - 1)
    else:
        f = pl.program_id(1)
        first = f == 0
        last = f == pl.num_programs(1) - 1

    @pl.when(first)
    def _():
        x = x_ref[...]
        if pre:
            x = x + ga_ref[...] * _bdot(ao_ref[...], wo_ref[...])
            xs_scr[...] = x
        h = _modnorm(x, g_ref[...], sh_ref[...], sc_ref[...])
        h_scr[...] = h.astype(BF16)
        if moe:
            logits = jnp.dot(h, wr_ref[...], precision=lax.Precision.HIGHEST,
                             preferred_element_type=F32) + br_ref[...]
            gates_scr[...] = _route_top2(logits, None)
        o_ref[...] = jnp.zeros(o_ref.shape, F32)

    hb = h_scr[...]
    acc = None
    for c0, cw in f_chunks:
        gg = _bdot(hb, wg_ref[:, c0:c0 + cw])
        uu = _bdot(hb, wu_ref[:, c0:c0 + cw])
        part = _bdot((_silu(gg) * uu).astype(BF16), wd_ref[c0:c0 + cw, :])
        acc = part if acc is None else acc + part
    if moe:
        gates = gates_scr[...]
        lane = lax.broadcasted_iota(jnp.int32, gates.shape, 1)
        acc = acc * jnp.sum(jnp.where(lane == e, gates, 0.0), -1, keepdims=True)
    o_ref[...] += acc

    @pl.when(last)
    def _():
        xb = xs_scr[...] if pre else x_ref[...]
        r = xb + gt_ref[...] * o_ref[...]
        if final:
            r = _rms(r, fg_ref[...])
        o_ref[...] = r


def _ffn(x, mods, g, wg, wu, wd, *, tm, attn=None, router=None, final_g=None):
    m, d = x.shape
    moe = router is not None
    pre = attn is not None
    final = final_g is not None
    nf_total = wg.shape[-1]
    tf = nf_total // 2
    assert tf % LANE == 0 and m % tm == 0
    f_chunks = tuple((c0, min(512, tf - c0)) for c0 in range(0, tf, 512))
    if moe:
        n_e = wg.shape[0]
        grid = (m // tm, n_e, nf_total // tf)
        wgu_spec = pl.BlockSpec((None, d, tf), lambda i, e, f: (e, 0, f))
        wd_spec = pl.BlockSpec((None, tf, d), lambda i, e, f: (e, f, 0))
        sem = ("parallel", "arbitrary", "arbitrary")
    else:
        grid = (m // tm, nf_total // tf)
        wgu_spec = pl.BlockSpec((d, tf), lambda i, f: (0, f))
        wd_spec = pl.BlockSpec((tf, d), lambda i, f: (f, 0))
        sem = ("parallel", "arbitrary")
    row_spec = pl.BlockSpec((tm, d), lambda i, *_: (i, 0))

    args, specs = [x], [row_spec]
    if pre:
        ao, wo, amods = attn
        args += [ao, wo, amods]
        specs += [row_spec, _const_spec(wo.shape), _mod_spec(amods, 2, d, tm)]
    args += [mods, mods, mods, g.reshape(1, d)]
    specs += [_mod_spec(mods, 3, d, tm), _mod_spec(mods, 4, d, tm), _mod_spec(mods, 5, d, tm),
              _const_spec((1, d))]
    if moe:
        args += list(router)
        specs += [_const_spec(router[0].shape), _const_spec(router[1].shape)]
    args += [wg, wu, wd]
    specs += [wgu_spec, wgu_spec, wd_spec]
    if final:
        args.append(final_g.reshape(1, d))
        specs.append(_const_spec((1, d)))
    scratch = [pltpu.VMEM((tm, d), BF16)]
    if pre:
        scratch.append(pltpu.VMEM((tm, d), F32))
    if moe:
        scratch.append(pltpu.VMEM((tm, LANE), F32))
    body = functools.partial(_ffn_body, moe=moe, pre=pre, final=final, f_chunks=f_chunks)
    return pl.pallas_call(
        body, grid=grid, in_specs=specs, out_specs=row_spec,
        out_shape=jax.ShapeDtypeStruct((m, d), F32),
        scratch_shapes=scratch, compiler_params=_cparams(sem),
        name="ffn_moe" if moe else "ffn_dense",
    )(*args)


ROUTE_CHUNK = 256


def _moe_routed_body(x_ref, ao_ref, wo_ref, ga_ref, sh_ref, sc_ref, gt_ref, g_ref, wr_ref, br_ref,
                     wg_ref, wu_ref, wd_ref, fg_ref, o_ref,
                     h_scr, xs_scr, gates_scr, rank_scr, rankt_scr, cnt_scr, hs_scr, oe_scr, *, f_chunks):
    tm = x_ref.shape[0]
    e, f = pl.program_id(1), pl.program_id(2)
    last_e, last_f = pl.num_programs(1) - 1, pl.num_programs(2) - 1
    gc = ROUTE_CHUNK

    @pl.when(jnp.logical_and(e == 0, f == 0))
    def _():
        x = x_ref[...] + ga_ref[...] * _bdot(ao_ref[...], wo_ref[...])
        xs_scr[...] = x
        h = _modnorm(x, g_ref[...], sh_ref[...], sc_ref[...])
        h_scr[...] = h.astype(BF16)
        logits = jnp.dot(h, wr_ref[...], precision=lax.Precision.HIGHEST,
                         preferred_element_type=F32) + br_ref[...]
        gates = _route_top2(logits, None)
        gates_scr[...] = gates
        member = gates > 0.0
        r_i = lax.broadcasted_iota(jnp.int32, (tm, tm), 0)
        c_i = lax.broadcasted_iota(jnp.int32, (tm, tm), 1)
        lower = jnp.where(c_i < r_i, 1.0, 0.0).astype(BF16)
        rank = _bdot(lower, jnp.where(member, 1.0, 0.0).astype(BF16))
        rank = jnp.where(member, rank, -1.0)
        rank_scr[...] = rank
        cnt_scr[...] = jnp.sum(jnp.where(member, 1.0, 0.0), 0, keepdims=True)
        rank_t = rank.T
        for ee in range(rankt_scr.shape[0]):
            rankt_scr[ee] = rank_t[ee:ee + 1, :]
        o_ref[...] = jnp.zeros(o_ref.shape, F32)

    lane = lax.broadcasted_iota(jnp.int32, (tm, LANE), 1)
    sel = lane == e
    n_rows = jnp.sum(jnp.where(sel[:1], cnt_scr[...], 0.0)).astype(jnp.int32)
    n_chunks = (n_rows + (gc - 1)) // gc
    rank_col = jnp.sum(jnp.where(sel, rank_scr[...], 0.0), -1, keepdims=True)
    gate_col = jnp.sum(jnp.where(sel, gates_scr[...], 0.0), -1, keepdims=True)
    rank_row = rankt_scr[e]

    def chunk(c, carry):
        r0 = pl.multiple_of(c * gc, gc)
        base = (c * gc).astype(F32)

        @pl.when(f == 0)
        def _():
            slot = lax.broadcasted_iota(jnp.int32, (gc, tm), 0).astype(F32) + base
            pick = jnp.where(rank_row == slot, 1.0, 0.0).astype(BF16)
            hs_scr[pl.ds(r0, gc), :] = _bdot(pick, h_scr[...]).astype(BF16)

        hc = hs_scr[pl.ds(r0, gc), :]
        acc = None
        for c0, cw in f_chunks:
            gg = _bdot(hc, wg_ref[:, c0:c0 + cw])
            uu = _bdot(hc, wu_ref[:, c0:c0 + cw])
            part = _bdot((_silu(gg) * uu).astype(BF16), wd_ref[c0:c0 + cw, :])
            acc = part if acc is None else acc + part

        @pl.when(f == 0)
        def _():
            oe_scr[pl.ds(r0, gc), :] = acc

        @pl.when(f != 0)
        def _():
            oe_scr[pl.ds(r0, gc), :] += acc

        @pl.when(f == last_f)
        def _():
            slot_t = lax.broadcasted_iota(jnp.int32, (tm, gc), 1).astype(F32) + base
            put = jnp.where(rank_col == slot_t, 1.0, 0.0).astype(BF16)
            o_ref[...] += gate_col * _bdot(put, oe_scr[pl.ds(r0, gc), :].astype(BF16))

        return carry

    lax.fori_loop(0, n_chunks, chunk, 0)

    @pl.when(jnp.logical_and(e == last_e, f == last_f))
    def _():
        o_ref[...] = _rms(xs_scr[...] + gt_ref[...] * o_ref[...], fg_ref[...])


def _moe_routed(x, ao, wo, amods, mods, g, router, wg, wu, wd, final_g, *, tm):
    m, d = x.shape
    n_e, _, nf_total = wg.shape
    tf = nf_total // 2
    assert tf % LANE == 0 and m % tm == 0 and tm % ROUTE_CHUNK == 0 and n_e <= SUBLANE
    f_chunks = tuple((c0, min(512, tf - c0)) for c0 in range(0, tf, 512))
    once = pl.Buffered(1)
    row = lambda: pl.BlockSpec((tm, d), lambda i, e, f: (i, 0), pipeline_mode=once)
    const = lambda shape: pl.BlockSpec(shape, lambda i, e, f: (0,) * len(shape), pipeline_mode=once)
    mod = lambda a, k: pl.BlockSpec((1, d), lambda i, e, f: (0, k), pipeline_mode=once)
    wr, br = router
    assert amods.shape[0] == 1 and mods.shape[0] == 1
    specs = [row(), row(), const(wo.shape), mod(amods, 2), mod(mods, 3), mod(mods, 4), mod(mods, 5),
             const((1, d)), const(wr.shape), const(br.shape),
             pl.BlockSpec((None, d, tf), lambda i, e, f: (e, 0, f)),
             pl.BlockSpec((None, d, tf), lambda i, e, f: (e, 0, f)),
             pl.BlockSpec((None, tf, d), lambda i, e, f: (e, f, 0)),
             const((1, d))]
    scratch = [pltpu.VMEM((tm, d), BF16), pltpu.VMEM((tm, d), F32), pltpu.VMEM((tm, LANE), F32),
               pltpu.VMEM((tm, LANE), F32), pltpu.VMEM((SUBLANE, 1, tm), F32), pltpu.VMEM((1, LANE), F32),
               pltpu.VMEM((tm, d), BF16), pltpu.VMEM((tm, d), F32)]
    body = functools.partial(_moe_routed_body, f_chunks=f_chunks)
    return pl.pallas_call(
        body, grid=(m // tm, n_e, nf_total // tf), in_specs=specs,
        out_specs=pl.BlockSpec((tm, d), lambda i, e, f: (i, 0)),
        out_shape=jax.ShapeDtypeStruct((m, d), F32), scratch_shapes=scratch,
        compiler_params=_cparams(("parallel", "arbitrary", "arbitrary")), name="moe_routed",
    )(x, ao, wo, amods, mods, mods, mods, g.reshape(1, d), wr, br, wg, wu, wd, final_g.reshape(1, d))


def _kvq_body(*refs, emit_kv, n_heads):
    (x_ref, ksh_ref, ksc_ref, kg_ref, wdkv_ref, kvg_ref, wkr_ref, wkrr_ref, cos32_ref, sin32_ref,
     qsh_ref, qsc_ref, qg_ref, wdq_ref, qng_ref, wq_ref, wqr_ref, cosq_ref, sinq_ref) = refs[:19]
    refs = refs[19:]
    if emit_kv:
        wk_ref, ek_ref, wuvt_ref, vone_ref = refs[:4]
        refs = refs[4:]
    ckv_o, kpe_o, q_o = refs[:3]
    x = x_ref[...]
    hk = _modnorm(x, kg_ref[...], ksh_ref[...], ksc_ref[...]).astype(BF16)
    ckv = _rms(_bdot(hk, wdkv_ref[...]), kvg_ref[...])
    ckv_o[...] = ckv
    kpe = _bdot(hk, wkr_ref[...]) * cos32_ref[...] + _bdot(hk, wkrr_ref[...]) * sin32_ref[...]
    kpe_o[...] = kpe

    hq = _modnorm(x, qg_ref[...], qsh_ref[...], qsc_ref[...]).astype(BF16)
    cq = _rms(_bdot(hq, wdq_ref[...]), qng_ref[...]).astype(BF16)
    q = _bdot(cq, wq_ref[...])
    qr = _bdot(cq, wqr_ref[...])
    cosq, sinq = cosq_ref[...], sinq_ref[...]
    for h in range(n_heads):
        sl = slice(h * HEAD_PAD, (h + 1) * HEAD_PAD)
        q_o[:, sl] = (q[:, sl] * cosq + qr[:, sl] * sinq).astype(BF16)
    if emit_kv:
        k_o, vt_o = refs[3:5]
        cb = ckv.astype(BF16)
        k_o[...] = (_bdot(cb, wk_ref[...]) + _bdot(kpe.astype(BF16), ek_ref[...])).astype(BF16)
        vt = lax.dot_general(wuvt_ref[...], cb, (((1,), (1,)), ((), ())), preferred_element_type=F32)
        vt_o[...] = (vt + vone_ref[...]).astype(BF16)


def _kvq(x, kv_mods, q_mods, w, tabs, *, tm, emit_kv):
    m, d = x.shape
    n_heads = w["wq"].shape[1] // HEAD_PAD
    kvl, rope = w["wdkv"].shape[1], w["wkr"].shape[1]
    row = lambda n: pl.BlockSpec((tm, n), lambda i: (i, 0))
    tab = lambda t: (pl.BlockSpec((1, t.shape[1]), lambda i: (0, 0)) if t.shape[0] == 1
                     else pl.BlockSpec((tm, t.shape[1]), lambda i: (i, 0)))
    cos32, sin32, cosq, sinq = tabs
    args = [x, kv_mods, kv_mods, w["kg"], w["wdkv"], w["kvg"], w["wkr"], w["wkrr"], cos32, sin32,
            q_mods, q_mods, w["qg"], w["wdq"], w["qng"], w["wq"], w["wqr"], cosq, sinq]
    specs = [row(d), _mod_spec(kv_mods, 0, d, tm), _mod_spec(kv_mods, 1, d, tm), _const_spec((1, d)),
             _const_spec(w["wdkv"].shape), _const_spec((1, kvl)), _const_spec(w["wkr"].shape),
             _const_spec(w["wkrr"].shape), tab(cos32), tab(sin32),
             _mod_spec(q_mods, 0, d, tm), _mod_spec(q_mods, 1, d, tm), _const_spec((1, d)),
             _const_spec(w["wdq"].shape), _const_spec(w["qng"].shape), _const_spec(w["wq"].shape),
             _const_spec(w["wqr"].shape), tab(cosq), tab(sinq)]
    out_shape = [jax.ShapeDtypeStruct((m, kvl), F32), jax.ShapeDtypeStruct((m, rope), F32),
                 jax.ShapeDtypeStruct((m, n_heads * HEAD_PAD), BF16)]
    out_specs = [row(kvl), row(rope), row(n_heads * HEAD_PAD)]
    if emit_kv:
        vrows = w["wuvt"].shape[0]
        args += [w["wk"], w["ek"], w["wuvt"], w["vone"]]
        specs += [_const_spec(w["wk"].shape), _const_spec(w["ek"].shape), _const_spec(w["wuvt"].shape),
                  _const_spec(w["vone"].shape)]
        out_shape += [jax.ShapeDtypeStruct((m, n_heads * HEAD_PAD), BF16),
                      jax.ShapeDtypeStruct((vrows, m), BF16)]
        out_specs += [row(n_heads * HEAD_PAD), pl.BlockSpec((vrows, tm), lambda i: (0, i))]
    body = functools.partial(_kvq_body, emit_kv=emit_kv, n_heads=n_heads)
    return pl.pallas_call(
        body, grid=(m // tm,), in_specs=specs, out_specs=out_specs, out_shape=out_shape,
        compiler_params=_cparams(("parallel",)), name="kvq",
    )(*args)


V_ROWS = 80


def _flash_body(q_ref, k_ref, vt_ref, o_ref, s_scr, p_scr, m_scr, acc_scr, *, tq, v_head):
    qi = pl.program_id(1)
    heads = (0, 1)
    for hh in heads:
        m_scr[hh] = jnp.full((1, tq), NEG, F32)
        acc_scr[hh] = jnp.zeros((V_ROWS, tq), F32)

    def qk(j, buf):
        k0 = pl.multiple_of(j * tq, tq)
        for hh in heads:
            q = q_ref[:, hh * HEAD_PAD:(hh + 1) * HEAD_PAD]
            k = k_ref[pl.ds(k0, tq), hh * HEAD_PAD:(hh + 1) * HEAD_PAD]
            s_scr[buf, hh] = lax.dot_general(k, q, (((1,), (1,)), ((), ())), preferred_element_type=F32)

    def softmax_pv(j, buf, masked):
        k0 = pl.multiple_of(j * tq, tq)
        alphas = []
        for hh in heads:
            if masked:
                krow = lax.broadcasted_iota(jnp.int32, (tq, tq), 0)
                qcol = lax.broadcasted_iota(jnp.int32, (tq, tq), 1)
                s_scr[buf, hh] = jnp.where(krow <= qcol, s_scr[buf, hh], NEG)
            m_old = m_scr[hh]
            m_new = jnp.maximum(m_old, jnp.max(s_scr[buf, hh], 0, keepdims=True))
            p_scr[hh] = jnp.exp2(s_scr[buf, hh] - m_new).astype(BF16)
            m_scr[hh] = m_new
            alphas.append(jnp.exp2(m_old - m_new))
        for hh in heads:
            vt = vt_ref[hh * V_ROWS:(hh + 1) * V_ROWS, pl.ds(k0, tq)]
            acc_scr[hh] = alphas[hh] * acc_scr[hh] + _bdot(vt, p_scr[hh])

    qk(qi, 0)
    softmax_pv(qi, 0, True)

    @pl.when(qi > 0)
    def _():
        qk(0, 0)

    def pair(i, carry):
        j = 2 * i
        qk(j + 1, 1)
        softmax_pv(j, 0, False)
        qk(j + 2, 0)
        softmax_pv(j + 1, 1, False)
        return carry

    lax.fori_loop(0, qi // 2, pair, 0)

    @pl.when(qi % 2 == 1)
    def _():
        softmax_pv(qi - 1, 0, False)

    outs = [acc_scr[hh, :v_head] / acc_scr[hh, v_head:v_head + 1] for hh in heads]
    o_ref[...] = jnp.concatenate(outs, 0).T.astype(BF16)


def _flash_prompt(q_all, k_all, vt_all, *, v_head):
    s = q_all.shape[0]
    n_pairs = q_all.shape[1] // (2 * HEAD_PAD)
    assert vt_all.shape[0] == n_pairs * 2 * V_ROWS and 2 * v_head == LANE and v_head < V_ROWS
    tq = 512
    body = functools.partial(_flash_body, tq=tq, v_head=v_head)
    return pl.pallas_call(
        body,
        grid=(n_pairs, s // tq),
        in_specs=[pl.BlockSpec((tq, 2 * HEAD_PAD), lambda hp, qi: (qi, hp)),
                  pl.BlockSpec((s, 2 * HEAD_PAD), lambda hp, qi: (0, hp)),
                  pl.BlockSpec((2 * V_ROWS, s), lambda hp, qi: (hp, 0))],
        out_specs=pl.BlockSpec((tq, 2 * v_head), lambda hp, qi: (qi, hp)),
        out_shape=jax.ShapeDtypeStruct((s, n_pairs * 2 * v_head), BF16),
        scratch_shapes=[pltpu.VMEM((2, 2, tq, tq), F32), pltpu.VMEM((2, tq, tq), BF16),
                        pltpu.VMEM((2, 1, tq), F32), pltpu.VMEM((2, V_ROWS, tq), F32)],
        compiler_params=_cparams(("parallel", "arbitrary")),
        name="flash_prompt",
    )(q_all, k_all, vt_all)


def _qlat_body(q_ref, w_ref, o_ref, *, n_heads, kvl):
    for h in range(n_heads):
        o_ref[:, h * kvl:(h + 1) * kvl] = _bdot(q_ref[:, h * HEAD_PAD:(h + 1) * HEAD_PAD], w_ref[h]).astype(BF16)


def _qlat(q_all, wukt):
    b = q_all.shape[0]
    n_heads, _, kvl = wukt.shape
    body = functools.partial(_qlat_body, n_heads=n_heads, kvl=kvl)
    return pl.pallas_call(
        body, grid=(1,),
        in_specs=[_const_spec(q_all.shape), _const_spec(wukt.shape)],
        out_specs=_const_spec((b, n_heads * kvl)),
        out_shape=jax.ShapeDtypeStruct((b, n_heads * kvl), BF16),
        compiler_params=_cparams(("arbitrary",)), name="q_latent",
    )(q_all, wukt)


def _ouv_body(o_ref, w_ref, out_ref, *, n_heads, kvl):
    for hp in range(n_heads // 2):
        acc = None
        for h in (2 * hp, 2 * hp + 1):
            t = _bdot(o_ref[:, h * kvl:(h + 1) * kvl].astype(BF16), w_ref[h])
            acc = t if acc is None else acc + t
        out_ref[:, hp * LANE:(hp + 1) * LANE] = acc.astype(BF16)


def _ouv(o_lat, wuvp):
    b = o_lat.shape[0]
    n_heads, kvl, _ = wuvp.shape
    body = functools.partial(_ouv_body, n_heads=n_heads, kvl=kvl)
    return pl.pallas_call(
        body, grid=(1,),
        in_specs=[_const_spec(o_lat.shape), _const_spec(wuvp.shape)],
        out_specs=_const_spec((b, n_heads // 2 * LANE)),
        out_shape=jax.ShapeDtypeStruct((b, n_heads // 2 * LANE), BF16),
        compiler_params=_cparams(("arbitrary",)), name="o_up",
    )(o_lat, wuvp)


PAGES_PER_CHUNK = 16


KEY_SPLIT = 4
N_SLOTS = 4
PREFETCH = N_SLOTS - 1


def _paged_body(pt_ref, ql_ref, qp_ref, cn_ref, kn_ref, cc_hbm, kt_hbm, o_ref, cbuf, kbuf, sem,
                *, n_pages, page):
    b = pl.program_id(0)
    nb = pl.num_programs(0)
    ch = PAGES_PER_CHUNK
    nch = n_pages // ch

    def copies(bb, c, slot):
        out = []
        for p in range(ch):
            pg = pt_ref[bb, c * ch + p]
            dst = pl.ds(p * page, page)
            out.append(pltpu.make_async_copy(cc_hbm.at[pg], cbuf.at[slot, dst], sem.at[0, slot]))
            out.append(pltpu.make_async_copy(kt_hbm.at[pg], kbuf.at[slot, :, dst], sem.at[1, slot]))
        return out

    def start(bb, c, slot):
        for cp in copies(bb, c, slot):
            cp.start()

    def wait(bb, c, slot):
        for cp in copies(bb, c, slot):
            cp.wait()

    @pl.when(b == 0)
    def _():
        for c in range(PREFETCH):
            start(0, c, c % N_SLOTS)

    def start_ahead(c, s):
        slot = (s + PREFETCH) % N_SLOTS
        if s + PREFETCH < N_SLOTS:
            start(b, c + PREFETCH, slot)
        else:
            @pl.when(c + PREFETCH < nch)
            def _():
                start(b, c + PREFETCH, slot)

            @pl.when(jnp.logical_and(c + PREFETCH >= nch, b + 1 < nb))
            def _():
                start(b + 1, c + PREFETCH - nch, slot)

    ql = ql_ref[...]
    qp = qp_ref[...]
    dn = (((1,), (1,)), ((), ()))

    sub = ch * page // KEY_SPLIT

    def attend(slot, carry):
        cks = [cbuf[slot, i * sub:(i + 1) * sub, :].astype(BF16) for i in range(KEY_SPLIT)]
        ss = [lax.dot_general(ql, cks[i], dn, preferred_element_type=F32)
              + _bdot(qp, kbuf[slot, :, i * sub:(i + 1) * sub].astype(BF16)) for i in range(KEY_SPLIT)]
        ms = [jnp.maximum(carry[i][0], jnp.max(ss[i], -1, keepdims=True)) for i in range(KEY_SPLIT)]
        ps = [jnp.exp2(ss[i] - ms[i]) for i in range(KEY_SPLIT)]
        pvs = [_bdot(ps[i].astype(BF16), cks[i]) for i in range(KEY_SPLIT)]
        new = []
        for i in range(KEY_SPLIT):
            m, l, acc = carry[i]
            alpha = jnp.exp2(m - ms[i])
            new.append((ms[i], alpha * l + jnp.sum(ps[i], -1, keepdims=True), alpha * acc + pvs[i]))
        return tuple(new)

    def ring(i, carry):
        for s in range(N_SLOTS):
            c = N_SLOTS * i + s
            wait(b, c, s)
            start_ahead(c, s)
            carry = attend(s, carry)
        return carry

    nh = ql.shape[0]
    one = (jnp.full((nh, 1), NEG, F32), jnp.zeros((nh, 1), F32), jnp.zeros((nh, ql.shape[1]), F32))
    states = lax.fori_loop(0, nch // N_SLOTS, ring, (one,) * KEY_SPLIT)

    cn = cn_ref[...]
    s_new = (jnp.sum(ql.astype(F32) * cn, -1, keepdims=True)
             + jnp.sum(qp.astype(F32) * kn_ref[...], -1, keepdims=True))
    m_f = s_new
    for m, _, _ in states:
        m_f = jnp.maximum(m_f, m)
    w_new = jnp.exp2(s_new - m_f)
    l_f = w_new
    acc_f = w_new * cn
    for m, l, acc in states:
        w = jnp.exp2(m - m_f)
        l_f = l_f + w * l
        acc_f = acc_f + w * acc
    o_ref[...] = acc_f / l_f


def _paged_attention(page_table, q_lat, q_pe, ckv_new, kpe_new, cache_ckv, cache_kpe_t):
    bsz, nh, kvl = q_lat.shape
    rope = q_pe.shape[-1]
    n_pages = page_table.shape[1]
    page = cache_ckv.shape[1]
    assert n_pages % (N_SLOTS * PAGES_PER_CHUNK) == 0 and (PAGES_PER_CHUNK * page) % (KEY_SPLIT * LANE) == 0
    rows = PAGES_PER_CHUNK * page
    blk = lambda a, b_: pl.BlockSpec((None, a, b_), lambda b, pt: (b, 0, 0))
    body = functools.partial(_paged_body, n_pages=n_pages, page=page)
    return pl.pallas_call(
        body,
        grid_spec=pltpu.PrefetchScalarGridSpec(
            num_scalar_prefetch=1, grid=(bsz,),
            in_specs=[blk(nh, kvl), blk(nh, rope), blk(1, kvl), blk(1, rope),
                      pl.BlockSpec(memory_space=pl.ANY), pl.BlockSpec(memory_space=pl.ANY)],
            out_specs=blk(nh, kvl),
            scratch_shapes=[pltpu.VMEM((N_SLOTS, rows, kvl), F32), pltpu.VMEM((N_SLOTS, rope, rows), F32),
                            pltpu.SemaphoreType.DMA((2, N_SLOTS))]),
        out_shape=jax.ShapeDtypeStruct((bsz, nh, kvl), F32),
        compiler_params=_cparams(("arbitrary",)),
        name="paged_attention",
    )(page_table, q_lat, q_pe, ckv_new, kpe_new, cache_ckv, cache_kpe_t)


def _rot_half_cols(w):
    half = w.shape[-1] // 2
    return jnp.concatenate([-w[..., half:], w[..., :half]], -1)


def _rope_tables(pos, rope, nope, scale):
    half = rope // 2
    inv = ROPE_THETA ** (-jnp.arange(half, dtype=F32) / half)
    ang = pos.astype(F32)[:, None] * inv[None, :]
    cos, sin = jnp.cos(ang), jnp.sin(ang)
    cos32 = jnp.concatenate([cos, cos], -1)
    sin32 = jnp.concatenate([sin, sin], -1)
    n = pos.shape[0]
    pad = HEAD_PAD - nope - rope
    cosq = jnp.concatenate([jnp.ones((n, nope), F32), cos32, jnp.zeros((n, pad), F32)], -1) * scale
    sinq = jnp.concatenate([jnp.zeros((n, nope), F32), sin32, jnp.zeros((n, pad), F32)], -1) * scale
    return cos32, sin32, cosq, sinq


def kernel(x_prompt, x_sample, c_prompt, c_sample, state_conv, cache_ckv, cache_kpe, page_table, ada_w, ada_b, norm_mix_g, norm_ffn_g, conv_w1, conv_b1, conv_wdw, conv_bdw, conv_ln_g, conv_ln_b, conv_w2, conv_b2, ada_kv_w, ada_kv_b, norm_kv_g, w_dkv, kv_norm_g, w_kr, w_uk, w_uv, w_dq, q_norm_g, w_uq, w_o, ffn_w_gate, ffn_w_up, ffn_w_down, router_w, router_b, moe_w_gate, moe_w_up, moe_w_down, final_norm_g):
    _, s, d = x_prompt.shape
    bd = x_sample.shape[0]
    assert x_prompt.shape[0] == 1 and x_sample.shape[1] == 1 and ada_w.shape[0] == 2
    kvl, n_heads, nope = w_uk.shape
    v_head = w_uv.shape[2]
    rope = w_kr.shape[1]
    q_lora = w_dq.shape[2]
    n_e = router_w.shape[2]
    page = cache_ckv.shape[1]
    past = page_table.shape[1] * page
    pad = HEAD_PAD - nope - rope
    sm_scale = float(nope + rope) ** -0.5 * LOG2_E

    w1 = conv_w1[0].astype(BF16)
    w2 = conv_w2[0].astype(BF16)
    wg0, wu0, wd0 = ffn_w_gate[0].astype(BF16), ffn_w_up[0].astype(BF16), ffn_w_down[0].astype(BF16)
    wge, wue, wde = moe_w_gate[0].astype(BF16), moe_w_up[0].astype(BF16), moe_w_down[0].astype(BF16)
    wo = w_o[0].astype(BF16)
    wr = jnp.pad(router_w[0], ((0, 0), (0, LANE - n_e)))
    br = jnp.pad(router_b[0], (0, LANE - n_e), constant_values=-jnp.inf).reshape(1, LANE)
    wuq = w_uq[0]
    wuq_rot = jnp.concatenate([jnp.zeros_like(wuq[..., :nope]), _rot_half_cols(wuq[..., nope:])], -1)
    pad_q = lambda w: jnp.pad(w, ((0, 0), (0, 0), (0, pad))).reshape(q_lora, n_heads * HEAD_PAD).astype(BF16)
    wk = jnp.pad(w_uk, ((0, 0), (0, 0), (0, HEAD_PAD - nope))).reshape(kvl, n_heads * HEAD_PAD).astype(BF16)
    ek = jnp.tile(jnp.pad(jnp.eye(rope, dtype=F32), ((0, 0), (nope, pad))), (1, n_heads)).astype(BF16)
    wuvt = jnp.pad(jnp.transpose(w_uv, (1, 2, 0)), ((0, 0), (0, V_ROWS - v_head), (0, 0)))
    wuvt = wuvt.reshape(n_heads * V_ROWS, kvl).astype(BF16)
    vone = jnp.tile((jnp.arange(V_ROWS) == v_head).astype(F32), n_heads).reshape(n_heads * V_ROWS, 1)
    kvq_w = dict(
        kg=norm_kv_g.reshape(1, d), wdkv=w_dkv.astype(BF16), kvg=kv_norm_g.reshape(1, kvl),
        wkr=w_kr.astype(BF16), wkrr=_rot_half_cols(w_kr).astype(BF16),
        qg=norm_mix_g[1].reshape(1, d), wdq=w_dq[0].astype(BF16), qng=q_norm_g[0].reshape(1, q_lora),
        wq=pad_q(wuq), wqr=pad_q(wuq_rot), wk=wk, ek=ek, wuvt=wuvt, vone=vone)
    wukt = jnp.pad(jnp.transpose(w_uk, (1, 2, 0)), ((0, 0), (0, HEAD_PAD - nope), (0, 0))).astype(BF16)
    wuv_h = jnp.transpose(w_uv, (1, 0, 2))
    wuvp = jnp.where((jnp.arange(n_heads) % 2 == 0)[:, None, None],
                     jnp.pad(wuv_h, ((0, 0), (0, 0), (0, v_head))),
                     jnp.pad(wuv_h, ((0, 0), (0, 0), (v_head, 0)))).astype(BF16)

    n_c = bd + 1
    c_all = jnp.pad(jnp.concatenate([c_sample, c_prompt], 0), ((0, -n_c % SUBLANE), (0, 0)))
    mods0 = _ada(c_all, ada_w, ada_b[0], layer=0)
    mods1 = _ada(c_all, ada_w, ada_b[1], layer=1)
    modskv = _ada(c_all, ada_kv_w, ada_kv_b)
    mp0, mp1, mpkv = mods0[bd:n_c], mods1[bd:n_c], modskv[bd:n_c]
    ms0, ms1, mskv = mods0[:bd], mods1[:bd], modskv[:bd]

    xp = x_prompt[0]
    x1, st_p = _mixer_prompt(xp, mp0, norm_mix_g[0], w1, conv_b1[0], conv_wdw[0], conv_bdw[0],
                             conv_ln_g[0], conv_ln_b[0], w2, conv_b2[0])
    x2 = _ffn(x1, mp0, norm_ffn_g[0], wg0, wu0, wd0, tm=1024)
    tabs_p = _rope_tables(jnp.arange(s, dtype=jnp.int32), rope, nope, sm_scale)
    ckv_p, kpe_p, q_p, k_p, vt_p = _kvq(x2, mpkv, mp1, kvq_w, tabs_p, tm=512, emit_kv=True)
    o_p = _flash_prompt(q_p, k_p, vt_p, v_head=v_head)
    y_p = _moe_routed(x2, o_p, wo, mp1, mp1, norm_ffn_g[1], (wr, br), wge, wue, wde, final_norm_g, tm=1024)

    xs = x_sample[:, 0]
    kw = conv_wdw.shape[1]
    state_t = jnp.transpose(state_conv[0], (1, 0, 2))
    xs1, u_s = _mixer_sample(xs, ms0, norm_mix_g[0], w1, conv_b1[0], state_t, conv_wdw[0], conv_bdw[0],
                             conv_ln_g[0], conv_ln_b[0], w2, conv_b2[0])
    xs2 = _ffn(xs1, ms0, norm_ffn_g[0], wg0, wu0, wd0, tm=bd)
    tabs_s = _rope_tables(jnp.full((1,), past, jnp.int32), rope, nope, sm_scale)
    ckv_s, kpe_s, q_s = _kvq(xs2, mskv, ms1, kvq_w, tabs_s, tm=bd, emit_kv=False)
    q_lat = _qlat(q_s, wukt).reshape(bd, n_heads, kvl)
    q_pe = q_s.reshape(bd, n_heads, HEAD_PAD)[:, :, nope:nope + rope]
    o_lat = _paged_attention(page_table, q_lat, q_pe, ckv_s.reshape(bd, 1, kvl), kpe_s.reshape(bd, 1, rope),
                             cache_ckv, jnp.transpose(cache_kpe, (0, 2, 1)))
    o_s = _ouv(o_lat.reshape(bd, n_heads * kvl), wuvp)
    y_s = _ffn(xs2, ms1, norm_ffn_g[1], wge, wue, wde, tm=bd, attn=(o_s, wo, ms1),
               router=(wr, br), final_g=final_norm_g)

    conv_state_prompt = st_p[CONV_HALO - (kw - 1):][None, None]
    conv_state_sample = jnp.concatenate([state_conv[0][:, 1:], u_s[:, None]], 1)[None]
    return (y_p[None], y_s[:, None], conv_state_prompt, conv_state_sample,
            ckv_p.reshape(1, s // page, page, kvl), kpe_p.reshape(1, s // page, page, rope),
            ckv_s[:, None], kpe_s[:, None])
```

```python
import functools

import jax
import jax.numpy as jnp
from jax import lax
from jax.experimental import pallas as pl
from jax.experimental.pallas import tpu as pltpu

F32 = jnp.float32
BF16 = jnp.bfloat16

NORM_EPS = 1e-6
ROPE_THETA = 10000.0
NEG = -1e30
N_MOD = 6
LOG2_E = 1.4426950408889634

V7X_VMEM_BYTES = 64 * 1024 * 1024
VMEM_LIMIT = 56 * 1024 * 1024
LANE = 128
SUBLANE = 8
HEAD_PAD = 128


def _cparams(sem):
    return pltpu.CompilerParams(dimension_semantics=sem, vmem_limit_bytes=VMEM_LIMIT)


def _rms(x, g):
    return x * lax.rsqrt(jnp.mean(x * x, -1, keepdims=True) + NORM_EPS) * g


def _modnorm(x, g, shift, scale):
    return _rms(x, g) * (1.0 + scale) + shift


def _silu(x):
    return x * jax.nn.sigmoid(x)


def _bdot(a, b):
    return jnp.dot(a, b, preferred_element_type=F32)


def _ada_body(c_ref, w_ref, b_ref, o_ref):
    s = _silu(c_ref[...]).astype(BF16)
    o_ref[...] = _bdot(s, w_ref[...].astype(BF16)) + b_ref[...]


def _ada(c_all, w, b, layer=None):
    m, d = c_all.shape
    n = w.shape[-1]
    tn = 1024
    if layer is None:
        w_spec = pl.BlockSpec((d, tn), lambda j: (0, j))
    else:
        w_spec = pl.BlockSpec((None, d, tn), lambda j: (layer, 0, j))
    return pl.pallas_call(
        _ada_body,
        grid=(n // tn,),
        in_specs=[pl.BlockSpec((m, d), lambda j: (0, 0)), w_spec,
                  pl.BlockSpec((1, tn), lambda j: (0, j))],
        out_specs=pl.BlockSpec((m, tn), lambda j: (0, j)),
        out_shape=jax.ShapeDtypeStruct((m, n), F32),
        compiler_params=_cparams(("parallel",)),
        name="ada_mod",
    )(c_all, w, b.reshape(1, n))


def _mod_spec(mods, k, d, tm):
    if mods.shape[0] == 1:
        return pl.BlockSpec((1, d), lambda i, *_: (0, k))
    return pl.BlockSpec((tm, d), lambda i, *_: (i, k))


def _const_spec(shape):
    nd = len(shape)
    return pl.BlockSpec(shape, lambda *_: (0,) * nd)


CONV_HALO = 32
CONV_TAIL = 16
CONV_RC = 32
CONV_CC = 256


def _layer_norm_silu(y, g, b):
    mu = jnp.mean(y, -1, keepdims=True)
    yc = y - mu
    var = jnp.mean(yc * yc, -1, keepdims=True)
    return _silu(yc * lax.rsqrt(var + NORM_EPS) * g + b)


def _mixer_prompt_body(x_ref, sh_ref, sc_ref, gt_ref, g_ref, w1_ref, b1_ref, wdw_ref, bdw_ref,
                       lng_ref, lnb_ref, w2_ref, b2_ref, o_ref, st_ref, ubuf, ybuf, *, tm, kw):
    d = x_ref.shape[-1]
    i = pl.program_id(0)

    @pl.when(i == 0)
    def _():
        ubuf[...] = jnp.zeros(ubuf.shape, F32)

    x = x_ref[...]
    h = _modnorm(x, g_ref[...], sh_ref[...], sc_ref[...])
    a = _bdot(h.astype(BF16), w1_ref[...]) + b1_ref[...]
    ubuf[CONV_HALO:CONV_HALO + tm, :] = a[:, :d] * jax.nn.sigmoid(a[:, d:])

    off = CONV_HALO - (kw - 1)
    nq = (kw - 1 + off) // SUBLANE + 1

    def chunk(rc, carry):
        r0 = pl.multiple_of(rc * CONV_RC, CONV_RC)
        for c0 in range(0, d, CONV_CC):
            win = ubuf[pl.ds(r0, CONV_RC + SUBLANE * (nq + 1)), c0:c0 + CONV_CC]
            y = None
            for r in range(SUBLANE):
                a_r = None
                for q in range(nq):
                    j = SUBLANE * q + r - off
                    if 0 <= j < kw:
                        t = wdw_ref[j:j + 1, c0:c0 + CONV_CC] * win[SUBLANE * q:SUBLANE * q + CONV_RC + SUBLANE]
                        a_r = t if a_r is None else a_r + t
                if a_r is not None:
                    s = a_r[r:r + CONV_RC]
                    y = s if y is None else y + s
            ybuf[pl.ds(r0, CONV_RC), c0:c0 + CONV_CC] = y + bdw_ref[:, c0:c0 + CONV_CC]
        return carry

    lax.fori_loop(0, tm // CONV_RC, chunk, 0)

    z = _layer_norm_silu(ybuf[...], lng_ref[...], lnb_ref[...])
    out = _bdot(z.astype(BF16), w2_ref[...]) + b2_ref[...]
    o_ref[...] = x + gt_ref[...] * out
    tail = ubuf[tm:tm + CONV_HALO, :]
    ubuf[0:CONV_HALO, :] = tail
    st_ref[...] = tail


def _mixer_prompt(x, mods, g, w1, b1, wdw, bdw, lng, lnb, w2, b2):
    s, d = x.shape
    kw = wdw.shape[0]
    tm = 512
    assert s % tm == 0 and kw - 1 <= CONV_HALO and s >= CONV_HALO
    body = functools.partial(_mixer_prompt_body, tm=tm, kw=kw)
    return pl.pallas_call(
        body,
        grid=(s // tm,),
        in_specs=[pl.BlockSpec((tm, d), lambda i: (i, 0)),
                  _mod_spec(mods, 0, d, tm), _mod_spec(mods, 1, d, tm), _mod_spec(mods, 2, d, tm),
                  _const_spec((1, d)), _const_spec(w1.shape), _const_spec((1, 2 * d)),
                  _const_spec(wdw.shape), _const_spec((1, d)), _const_spec((1, d)), _const_spec((1, d)),
                  _const_spec(w2.shape), _const_spec((1, d))],
        out_specs=[pl.BlockSpec((tm, d), lambda i: (i, 0)), _const_spec((CONV_HALO, d))],
        out_shape=[jax.ShapeDtypeStruct((s, d), F32), jax.ShapeDtypeStruct((CONV_HALO, d), F32)],
        scratch_shapes=[pltpu.VMEM((CONV_HALO + tm + CONV_TAIL, d), F32), pltpu.VMEM((tm, d), F32)],
        compiler_params=_cparams(("arbitrary",)),
        name="mixer_prompt",
    )(x, mods, mods, mods, g.reshape(1, d), w1, b1.reshape(1, -1), wdw, bdw.reshape(1, d),
      lng.reshape(1, d), lnb.reshape(1, d), w2, b2.reshape(1, d))


def _mixer_sample_body(x_ref, sh_ref, sc_ref, gt_ref, g_ref, w1_ref, b1_ref, st_ref, wdw_ref, bdw_ref,
                       lng_ref, lnb_ref, w2_ref, b2_ref, o_ref, u_ref, *, kw):
    d = x_ref.shape[-1]
    x = x_ref[...]
    h = _modnorm(x, g_ref[...], sh_ref[...], sc_ref[...])
    a = _bdot(h.astype(BF16), w1_ref[...]) + b1_ref[...]
    u = a[:, :d] * jax.nn.sigmoid(a[:, d:])
    u_ref[...] = u
    y = bdw_ref[...] + wdw_ref[kw - 1:kw, :] * u
    for j in range(kw - 1):
        y = y + wdw_ref[j:j + 1, :] * st_ref[j]
    z = _layer_norm_silu(y, lng_ref[...], lnb_ref[...])
    out = _bdot(z.astype(BF16), w2_ref[...]) + b2_ref[...]
    o_ref[...] = x + gt_ref[...] * out


def _mixer_sample(x, mods, g, w1, b1, state_t, wdw, bdw, lng, lnb, w2, b2):
    bsz, d = x.shape
    kw = wdw.shape[0]
    tb = 32
    body = functools.partial(_mixer_sample_body, kw=kw)
    return pl.pallas_call(
        body,
        grid=(bsz // tb,),
        in_specs=[pl.BlockSpec((tb, d), lambda i: (i, 0)),
                  _mod_spec(mods, 0, d, tb), _mod_spec(mods, 1, d, tb), _mod_spec(mods, 2, d, tb),
                  _const_spec((1, d)), _const_spec(w1.shape), _const_spec((1, 2 * d)),
                  pl.BlockSpec((kw - 1, tb, d), lambda i: (0, i, 0)),
                  _const_spec(wdw.shape), _const_spec((1, d)), _const_spec((1, d)), _const_spec((1, d)),
                  _const_spec(w2.shape), _const_spec((1, d))],
        out_specs=[pl.BlockSpec((tb, d), lambda i: (i, 0)), pl.BlockSpec((tb, d), lambda i: (i, 0))],
        out_shape=[jax.ShapeDtypeStruct((bsz, d), F32), jax.ShapeDtypeStruct((bsz, d), F32)],
        compiler_params=_cparams(("parallel",)),
        name="mixer_sample",
    )(x, mods, mods, mods, g.reshape(1, d), w1, b1.reshape(1, -1), state_t, wdw, bdw.reshape(1, d),
      lng.reshape(1, d), lnb.reshape(1, d), w2, b2.reshape(1, d))


def _route_top2(logits, n_e):
    lane = lax.broadcasted_iota(jnp.int32, logits.shape, 1)
    m1 = jnp.max(logits, -1, keepdims=True)
    i1 = jnp.min(jnp.where(logits == m1, lane, LANE), -1, keepdims=True)
    sel1 = lane == i1
    rest = jnp.where(sel1, -jnp.inf, logits)
    m2 = jnp.max(rest, -1, keepdims=True)
    i2 = jnp.min(jnp.where(rest == m2, lane, LANE), -1, keepdims=True)
    sel2 = lane == i2
    e2 = jnp.exp(m2 - m1)
    den = 1.0 + e2
    return jnp.where(sel1, 1.0 / den, 0.0) + jnp.where(sel2, e2 / den, 0.0)


def _ffn_body(*refs, moe, pre, final, f_chunks):
    refs = list(refs)
    x_ref = refs.pop(0)
    if pre:
        ao_ref, wo_ref, ga_ref = refs.pop(0), refs.pop(0), refs.pop(0)
    sh_ref, sc_ref, gt_ref, g_ref = refs.pop(0), refs.pop(0), refs.pop(0), refs.pop(0)
    if moe:
        wr_ref, br_ref = refs.pop(0), refs.pop(0)
    wg_ref, wu_ref, wd_ref = refs.pop(0), refs.pop(0), refs.pop(0)
    if final:
        fg_ref = refs.pop(0)
    o_ref = refs.pop(0)
    h_scr = refs.pop(0)
    if pre:
        xs_scr = refs.pop(0)
    if moe:
        gates_scr = refs.pop(0)

    if moe:
        e, f = pl.program_id(1), pl.program_id(2)
        first = jnp.logical_and(e == 0, f == 0)
        last = jnp.logical_and(e == pl.num_programs(1) - 1, f == pl.num_programs(2) - 1)
    else:
        f = pl.program_id(1)
        first = f == 0
        last = f == pl.num_programs(1) - 1

    @pl.when(first)
    def _():
        x = x_ref[...]
        if pre:
            x = x + ga_ref[...] * _bdot(ao_ref[...], wo_ref[...])
            xs_scr[...] = x
        h = _modnorm(x, g_ref[...], sh_ref[...], sc_ref[...])
        h_scr[...] = h.astype(BF16)
        if moe:
            logits = jnp.dot(h, wr_ref[...], precision=lax.Precision.HIGHEST,
                             preferred_element_type=F32) + br_ref[...]
            gates_scr[...] = _route_top2(logits, None)
        o_ref[...] = jnp.zeros(o_ref.shape, F32)

    hb = h_scr[...]
    acc = None
    for c0, cw in f_chunks:
        gg = _bdot(hb, wg_ref[:, c0:c0 + cw])
        uu = _bdot(hb, wu_ref[:, c0:c0 + cw])
        part = _bdot((_silu(gg) * uu).astype(BF16), wd_ref[c0:c0 + cw, :])
        acc = part if acc is None else acc + part
    if moe:
        gates = gates_scr[...]
        lane = lax.broadcasted_iota(jnp.int32, gates.shape, 1)
        acc = acc * jnp.sum(jnp.where(lane == e, gates, 0.0), -1, keepdims=True)
    o_ref[...] += acc

    @pl.when(last)
    def _():
        xb = xs_scr[...] if pre else x_ref[...]
        r = xb + gt_ref[...] * o_ref[...]
        if final:
            r = _rms(r, fg_ref[...])
        o_ref[...] = r


def _ffn(x, mods, g, wg, wu, wd, *, tm, attn=None, router=None, final_g=None):
    m, d = x.shape
    moe = router is not None
    pre = attn is not None
    final = final_g is not None
    nf_total = wg.shape[-1]
    tf = nf_total // 2
    assert tf % LANE == 0 and m % tm == 0
    f_chunks = tuple((c0, min(512, tf - c0)) for c0 in range(0, tf, 512))
    if moe:
        n_e = wg.shape[0]
        grid = (m // tm, n_e, nf_total // tf)
        wgu_spec = pl.BlockSpec((None, d, tf), lambda i, e, f: (e, 0, f))
        wd_spec = pl.BlockSpec((None, tf, d), lambda i, e, f: (e, f, 0))
        sem = ("parallel", "arbitrary", "arbitrary")
    else:
        grid = (m // tm, nf_total // tf)
        wgu_spec = pl.BlockSpec((d, tf), lambda i, f: (0, f))
        wd_spec = pl.BlockSpec((tf, d), lambda i, f: (f, 0))
        sem = ("parallel", "arbitrary")
    row_spec = pl.BlockSpec((tm, d), lambda i, *_: (i, 0))

    args, specs = [x], [row_spec]
    if pre:
        ao, wo, amods = attn
        args += [ao, wo, amods]
        specs += [row_spec, _const_spec(wo.shape), _mod_spec(amods, 2, d, tm)]
    args += [mods, mods, mods, g.reshape(1, d)]
    specs += [_mod_spec(mods, 3, d, tm), _mod_spec(mods, 4, d, tm), _mod_spec(mods, 5, d, tm),
              _const_spec((1, d))]
    if moe:
        args += list(router)
        specs += [_const_spec(router[0].shape), _const_spec(router[1].shape)]
    args += [wg, wu, wd]
    specs += [wgu_spec, wgu_spec, wd_spec]
    if final:
        args.append(final_g.reshape(1, d))
        specs.append(_const_spec((1, d)))
    scratch = [pltpu.VMEM((tm, d), BF16)]
    if pre:
        scratch.append(pltpu.VMEM((tm, d), F32))
    if moe:
        scratch.append(pltpu.VMEM((tm, LANE), F32))
    body = functools.partial(_ffn_body, moe=moe, pre=pre, final=final, f_chunks=f_chunks)
    return pl.pallas_call(
        body, grid=grid, in_specs=specs, out_specs=row_spec,
        out_shape=jax.ShapeDtypeStruct((m, d), F32),
        scratch_shapes=scratch, compiler_params=_cparams(sem),
        name="ffn_moe" if moe else "ffn_dense",
    )(*args)


ROUTE_CHUNK = 256


def _moe_routed_body(x_ref, ao_ref, wo_ref, ga_ref, sh_ref, sc_ref, gt_ref, g_ref, wr_ref, br_ref,
                     wga_ref, wgb_ref, wua_ref, wub_ref, wda_ref, wdb_ref, fg_ref, o_ref,
                     h_scr, xs_scr, gates_scr, rank_scr, rankt_scr, cnt_scr, hs_scr, oe_scr, *, f_chunks):
    tm = x_ref.shape[0]
    e, f = pl.program_id(1), pl.program_id(2)
    last_e, last_f = pl.num_programs(1) - 1, pl.num_programs(2) - 1
    gc = ROUTE_CHUNK

    @pl.when(jnp.logical_and(e == 0, f == 0))
    def _():
        x = x_ref[...] + ga_ref[...] * _bdot(ao_ref[...], wo_ref[...])
        xs_scr[...] = x
        h = _modnorm(x, g_ref[...], sh_ref[...], sc_ref[...])
        h_scr[...] = h.astype(BF16)
        logits = jnp.dot(h, wr_ref[...], precision=lax.Precision.HIGHEST,
                         preferred_element_type=F32) + br_ref[...]
        gates = _route_top2(logits, None)
        gates_scr[...] = gates
        member = gates > 0.0
        r_i = lax.broadcasted_iota(jnp.int32, (tm, tm), 0)
        c_i = lax.broadcasted_iota(jnp.int32, (tm, tm), 1)
        lower = jnp.where(c_i < r_i, 1.0, 0.0).astype(BF16)
        rank = _bdot(lower, jnp.where(member, 1.0, 0.0).astype(BF16))
        rank = jnp.where(member, rank, -1.0)
        rank_scr[...] = rank
        cnt_scr[...] = jnp.sum(jnp.where(member, 1.0, 0.0), 0, keepdims=True)
        rank_t = rank.T
        for ee in range(rankt_scr.shape[0]):
            rankt_scr[ee] = rank_t[ee:ee + 1, :]
        o_ref[...] = jnp.zeros(o_ref.shape, F32)

    lane = lax.broadcasted_iota(jnp.int32, (tm, LANE), 1)
    sel = lane == e
    n_rows = jnp.sum(jnp.where(sel[:1], cnt_scr[...], 0.0)).astype(jnp.int32)
    n_chunks = (n_rows + (gc - 1)) // gc
    rank_col = jnp.sum(jnp.where(sel, rank_scr[...], 0.0), -1, keepdims=True)
    gate_col = jnp.sum(jnp.where(sel, gates_scr[...], 0.0), -1, keepdims=True)
    rank_row = rankt_scr[e]

    def chunk(c, carry):
        r0 = pl.multiple_of(c * gc, gc)
        base = (c * gc).astype(F32)

        @pl.when(f == 0)
        def _():
            slot = lax.broadcasted_iota(jnp.int32, (gc, tm), 0).astype(F32) + base
            pick = jnp.where(rank_row == slot, 1.0, 0.0).astype(BF16)
            hs_scr[pl.ds(r0, gc), :] = _bdot(pick, h_scr[...]).astype(BF16)

        dh = hs_scr.shape[1] // 2
        ha = hs_scr[pl.ds(r0, gc), :dh]
        hb = hs_scr[pl.ds(r0, gc), dh:]
        acc_a = acc_b = None
        for c0, cw in f_chunks:
            gg = _bdot(ha, wga_ref[:, c0:c0 + cw]) + _bdot(hb, wgb_ref[:, c0:c0 + cw])
            uu = _bdot(ha, wua_ref[:, c0:c0 + cw]) + _bdot(hb, wub_ref[:, c0:c0 + cw])
            act = (_silu(gg) * uu).astype(BF16)
            pa = _bdot(act, wda_ref[c0:c0 + cw, :])
            pb = _bdot(act, wdb_ref[c0:c0 + cw, :])
            acc_a = pa if acc_a is None else acc_a + pa
            acc_b = pb if acc_b is None else acc_b + pb
        acc = jnp.concatenate([acc_a, acc_b], -1)

        @pl.when(f == 0)
        def _():
            oe_scr[pl.ds(r0, gc), :] = acc

        @pl.when(f != 0)
        def _():
            oe_scr[pl.ds(r0, gc), :] += acc

        @pl.when(f == last_f)
        def _():
            slot_t = lax.broadcasted_iota(jnp.int32, (tm, gc), 1).astype(F32) + base
            put = jnp.where(rank_col == slot_t, 1.0, 0.0).astype(BF16)
            o_ref[...] += gate_col * _bdot(put, oe_scr[pl.ds(r0, gc), :].astype(BF16))

        return carry

    lax.fori_loop(0, n_chunks, chunk, 0)

    @pl.when(jnp.logical_and(e == last_e, f == last_f))
    def _():
        o_ref[...] = _rms(xs_scr[...] + gt_ref[...] * o_ref[...], fg_ref[...])


def _moe_routed(x, ao, wo, amods, mods, g, router, wg, wu, wd, final_g, *, tm):
    m, d = x.shape
    n_e, _, nf_total = wg.shape
    tf = nf_total // 2
    assert tf % LANE == 0 and m % tm == 0 and tm % ROUTE_CHUNK == 0 and n_e <= SUBLANE
    f_chunks = tuple((c0, min(512, tf - c0)) for c0 in range(0, tf, 512))
    once = pl.Buffered(1)
    up = lambda half: pl.BlockSpec((None, d // 2, tf), lambda i, e, f: (e, half, f))
    down = lambda half: pl.BlockSpec((None, tf, d // 2), lambda i, e, f: (e, f, half))
    row = lambda: pl.BlockSpec((tm, d), lambda i, e, f: (i, 0), pipeline_mode=once)
    const = lambda shape: pl.BlockSpec(shape, lambda i, e, f: (0,) * len(shape), pipeline_mode=once)
    mod = lambda a, k: pl.BlockSpec((1, d), lambda i, e, f: (0, k), pipeline_mode=once)
    wr, br = router
    assert amods.shape[0] == 1 and mods.shape[0] == 1
    specs = [row(), row(), const(wo.shape), mod(amods, 2), mod(mods, 3), mod(mods, 4), mod(mods, 5),
             const((1, d)), const(wr.shape), const(br.shape),
             up(0), up(1), up(0), up(1), down(0), down(1),
             const((1, d))]
    scratch = [pltpu.VMEM((tm, d), BF16), pltpu.VMEM((tm, d), F32), pltpu.VMEM((tm, LANE), F32),
               pltpu.VMEM((tm, LANE), F32), pltpu.VMEM((SUBLANE, 1, tm), F32), pltpu.VMEM((1, LANE), F32),
               pltpu.VMEM((tm, d), BF16), pltpu.VMEM((tm, d), F32)]
    body = functools.partial(_moe_routed_body, f_chunks=f_chunks)
    return pl.pallas_call(
        body, grid=(m // tm, n_e, nf_total // tf), in_specs=specs,
        out_specs=pl.BlockSpec((tm, d), lambda i, e, f: (i, 0)),
        out_shape=jax.ShapeDtypeStruct((m, d), F32), scratch_shapes=scratch,
        compiler_params=_cparams(("parallel", "arbitrary", "arbitrary")), name="moe_routed",
    )(x, ao, wo, amods, mods, mods, mods, g.reshape(1, d), wr, br, wg, wg, wu, wu, wd, wd, final_g.reshape(1, d))


def _kvq_body(*refs, emit_kv, n_heads):
    (x_ref, ksh_ref, ksc_ref, kg_ref, wdkv_ref, kvg_ref, wkr_ref, wkrr_ref, cos32_ref, sin32_ref,
     qsh_ref, qsc_ref, qg_ref, wdq_ref, qng_ref, wq_ref, wqr_ref, cosq_ref, sinq_ref) = refs[:19]
    refs = refs[19:]
    if emit_kv:
        wk_ref, ek_ref, wuvt_ref, vone_ref = refs[:4]
        refs = refs[4:]
    ckv_o, kpe_o, q_o = refs[:3]
    x = x_ref[...]
    hk = _modnorm(x, kg_ref[...], ksh_ref[...], ksc_ref[...]).astype(BF16)
    ckv = _rms(_bdot(hk, wdkv_ref[...]), kvg_ref[...])
    ckv_o[...] = ckv
    kpe = _bdot(hk, wkr_ref[...]) * cos32_ref[...] + _bdot(hk, wkrr_ref[...]) * sin32_ref[...]
    kpe_o[...] = kpe

    hq = _modnorm(x, qg_ref[...], qsh_ref[...], qsc_ref[...]).astype(BF16)
    cq = _rms(_bdot(hq, wdq_ref[...]), qng_ref[...]).astype(BF16)
    q = _bdot(cq, wq_ref[...])
    qr = _bdot(cq, wqr_ref[...])
    cosq, sinq = cosq_ref[...], sinq_ref[...]
    for h in range(n_heads):
        sl = slice(h * HEAD_PAD, (h + 1) * HEAD_PAD)
        q_o[:, sl] = (q[:, sl] * cosq + qr[:, sl] * sinq).astype(BF16)
    if emit_kv:
        k_o, vt_o = refs[3:5]
        cb = ckv.astype(BF16)
        k_o[...] = (_bdot(cb, wk_ref[...]) + _bdot(kpe.astype(BF16), ek_ref[...])).astype(BF16)
        vt = lax.dot_general(wuvt_ref[...], cb, (((1,), (1,)), ((), ())), preferred_element_type=F32)
        vt_o[...] = (vt + vone_ref[...]).astype(BF16)


def _kvq(x, kv_mods, q_mods, w, tabs, *, tm, emit_kv):
    m, d = x.shape
    n_heads = w["wq"].shape[1] // HEAD_PAD
    kvl, rope = w["wdkv"].shape[1], w["wkr"].shape[1]
    row = lambda n: pl.BlockSpec((tm, n), lambda i: (i, 0))
    tab = lambda t: (pl.BlockSpec((1, t.shape[1]), lambda i: (0, 0)) if t.shape[0] == 1
                     else pl.BlockSpec((tm, t.shape[1]), lambda i: (i, 0)))
    cos32, sin32, cosq, sinq = tabs
    args = [x, kv_mods, kv_mods, w["kg"], w["wdkv"], w["kvg"], w["wkr"], w["wkrr"], cos32, sin32,
            q_mods, q_mods, w["qg"], w["wdq"], w["qng"], w["wq"], w["wqr"], cosq, sinq]
    specs = [row(d), _mod_spec(kv_mods, 0, d, tm), _mod_spec(kv_mods, 1, d, tm), _const_spec((1, d)),
             _const_spec(w["wdkv"].shape), _const_spec((1, kvl)), _const_spec(w["wkr"].shape),
             _const_spec(w["wkrr"].shape), tab(cos32), tab(sin32),
             _mod_spec(q_mods, 0, d, tm), _mod_spec(q_mods, 1, d, tm), _const_spec((1, d)),
             _const_spec(w["wdq"].shape), _const_spec(w["qng"].shape), _const_spec(w["wq"].shape),
             _const_spec(w["wqr"].shape), tab(cosq), tab(sinq)]
    out_shape = [jax.ShapeDtypeStruct((m, kvl), F32), jax.ShapeDtypeStruct((m, rope), F32),
                 jax.ShapeDtypeStruct((m, n_heads * HEAD_PAD), BF16)]
    out_specs = [row(kvl), row(rope), row(n_heads * HEAD_PAD)]
    if emit_kv:
        vrows = w["wuvt"].shape[0]
        args += [w["wk"], w["ek"], w["wuvt"], w["vone"]]
        specs += [_const_spec(w["wk"].shape), _const_spec(w["ek"].shape), _const_spec(w["wuvt"].shape),
                  _const_spec(w["vone"].shape)]
        out_shape += [jax.ShapeDtypeStruct((m, n_heads * HEAD_PAD), BF16),
                      jax.ShapeDtypeStruct((vrows, m), BF16)]
        out_specs += [row(n_heads * HEAD_PAD), pl.BlockSpec((vrows, tm), lambda i: (0, i))]
    body = functools.partial(_kvq_body, emit_kv=emit_kv, n_heads=n_heads)
    return pl.pallas_call(
        body, grid=(m // tm,), in_specs=specs, out_specs=out_specs, out_shape=out_shape,
        compiler_params=_cparams(("parallel",)), name="kvq",
    )(*args)


V_ROWS = 80


def _flash_body(q_ref, k_ref, vt_ref, o_ref, s_scr, p_scr, m_scr, acc_scr, *, tq, v_head):
    qi = pl.program_id(1)
    heads = (0, 1)
    for hh in heads:
        m_scr[hh] = jnp.full((1, tq), NEG, F32)
        acc_scr[hh] = jnp.zeros((V_ROWS, tq), F32)

    def qk(j, buf):
        k0 = pl.multiple_of(j * tq, tq)
        for hh in heads:
            q = q_ref[:, hh * HEAD_PAD:(hh + 1) * HEAD_PAD]
            k = k_ref[pl.ds(k0, tq), hh * HEAD_PAD:(hh + 1) * HEAD_PAD]
            s_scr[buf, hh] = lax.dot_general(k, q, (((1,), (1,)), ((), ())), preferred_element_type=F32)

    def softmax_pv(j, buf, masked):
        k0 = pl.multiple_of(j * tq, tq)
        alphas = []
        for hh in heads:
            if masked:
                krow = lax.broadcasted_iota(jnp.int32, (tq, tq), 0)
                qcol = lax.broadcasted_iota(jnp.int32, (tq, tq), 1)
                s_scr[buf, hh] = jnp.where(krow <= qcol, s_scr[buf, hh], NEG)
            m_old = m_scr[hh]
            m_new = jnp.maximum(m_old, jnp.max(s_scr[buf, hh], 0, keepdims=True))
            p_scr[hh] = jnp.exp2(s_scr[buf, hh] - m_new).astype(BF16)
            m_scr[hh] = m_new
            alphas.append(jnp.exp2(m_old - m_new))
        for hh in heads:
            vt = vt_ref[hh * V_ROWS:(hh + 1) * V_ROWS, pl.ds(k0, tq)]
            acc_scr[hh] = alphas[hh] * acc_scr[hh] + _bdot(vt, p_scr[hh])

    qk(qi, 0)
    softmax_pv(qi, 0, True)

    @pl.when(qi > 0)
    def _():
        qk(0, 0)

    def pair(i, carry):
        j = 2 * i
        qk(j + 1, 1)
        softmax_pv(j, 0, False)
        qk(j + 2, 0)
        softmax_pv(j + 1, 1, False)
        return carry

    lax.fori_loop(0, qi // 2, pair, 0)

    @pl.when(qi % 2 == 1)
    def _():
        softmax_pv(qi - 1, 0, False)

    outs = [acc_scr[hh, :v_head] / acc_scr[hh, v_head:v_head + 1] for hh in heads]
    o_ref[...] = jnp.concatenate(outs, 0).T.astype(BF16)


def _flash_prompt(q_all, k_all, vt_all, *, v_head):
    s = q_all.shape[0]
    n_pairs = q_all.shape[1] // (2 * HEAD_PAD)
    assert vt_all.shape[0] == n_pairs * 2 * V_ROWS and 2 * v_head == LANE and v_head < V_ROWS
    tq = 512
    body = functools.partial(_flash_body, tq=tq, v_head=v_head)
    return pl.pallas_call(
        body,
        grid=(n_pairs, s // tq),
        in_specs=[pl.BlockSpec((tq, 2 * HEAD_PAD), lambda hp, qi: (qi, hp)),
                  pl.BlockSpec((s, 2 * HEAD_PAD), lambda hp, qi: (0, hp)),
                  pl.BlockSpec((2 * V_ROWS, s), lambda hp, qi: (hp, 0))],
        out_specs=pl.BlockSpec((tq, 2 * v_head), lambda hp, qi: (qi, hp)),
        out_shape=jax.ShapeDtypeStruct((s, n_pairs * 2 * v_head), BF16),
        scratch_shapes=[pltpu.VMEM((2, 2, tq, tq), F32), pltpu.VMEM((2, tq, tq), BF16),
                        pltpu.VMEM((2, 1, tq), F32), pltpu.VMEM((2, V_ROWS, tq), F32)],
        compiler_params=_cparams(("parallel", "arbitrary")),
        name="flash_prompt",
    )(q_all, k_all, vt_all)


def _qlat_body(q_ref, w_ref, o_ref, *, n_heads, kvl):
    for h in range(n_heads):
        o_ref[:, h * kvl:(h + 1) * kvl] = _bdot(q_ref[:, h * HEAD_PAD:(h + 1) * HEAD_PAD], w_ref[h]).astype(BF16)


def _qlat(q_all, wukt):
    b = q_all.shape[0]
    n_heads, _, kvl = wukt.shape
    body = functools.partial(_qlat_body, n_heads=n_heads, kvl=kvl)
    return pl.pallas_call(
        body, grid=(1,),
        in_specs=[_const_spec(q_all.shape), _const_spec(wukt.shape)],
        out_specs=_const_spec((b, n_heads * kvl)),
        out_shape=jax.ShapeDtypeStruct((b, n_heads * kvl), BF16),
        compiler_params=_cparams(("arbitrary",)), name="q_latent",
    )(q_all, wukt)


def _ouv_body(o_ref, w_ref, out_ref, *, n_heads, kvl):
    for hp in range(n_heads // 2):
        acc = None
        for h in (2 * hp, 2 * hp + 1):
            t = _bdot(o_ref[:, h * kvl:(h + 1) * kvl].astype(BF16), w_ref[h])
            acc = t if acc is None else acc + t
        out_ref[:, hp * LANE:(hp + 1) * LANE] = acc.astype(BF16)


def _ouv(o_lat, wuvp):
    b = o_lat.shape[0]
    n_heads, kvl, _ = wuvp.shape
    body = functools.partial(_ouv_body, n_heads=n_heads, kvl=kvl)
    return pl.pallas_call(
        body, grid=(1,),
        in_specs=[_const_spec(o_lat.shape), _const_spec(wuvp.shape)],
        out_specs=_const_spec((b, n_heads // 2 * LANE)),
        out_shape=jax.ShapeDtypeStruct((b, n_heads // 2 * LANE), BF16),
        compiler_params=_cparams(("arbitrary",)), name="o_up",
    )(o_lat, wuvp)


PAGES_PER_CHUNK = 16


KEY_SPLIT = 4
N_SLOTS = 8
PREFETCH = N_SLOTS - 1


def _paged_body(pt_ref, ql_ref, qp_ref, cn_ref, kn_ref, cc_hbm, kt_hbm, o_ref, cbuf, kbuf, sem,
                *, n_pages, page):
    b = pl.program_id(0)
    nb = pl.num_programs(0)
    ch = PAGES_PER_CHUNK
    nch = n_pages // ch

    def copies(bb, c, slot):
        out = []
        for p in range(ch):
            pg = pt_ref[bb, c * ch + p]
            dst = pl.ds(p * page, page)
            out.append(pltpu.make_async_copy(cc_hbm.at[pg], cbuf.at[slot, dst], sem.at[0, slot]))
            out.append(pltpu.make_async_copy(kt_hbm.at[pg], kbuf.at[slot, :, dst], sem.at[1, slot]))
        return out

    def start(bb, c, slot):
        for cp in copies(bb, c, slot):
            cp.start()

    def wait(bb, c, slot):
        for cp in copies(bb, c, slot):
            cp.wait()

    @pl.when(b == 0)
    def _():
        for c in range(PREFETCH):
            start(0, c, c % N_SLOTS)

    def start_ahead(c, s):
        slot = (s + PREFETCH) % N_SLOTS
        if s + PREFETCH < N_SLOTS:
            start(b, c + PREFETCH, slot)
        else:
            @pl.when(c + PREFETCH < nch)
            def _():
                start(b, c + PREFETCH, slot)

            @pl.when(jnp.logical_and(c + PREFETCH >= nch, b + 1 < nb))
            def _():
                start(b + 1, c + PREFETCH - nch, slot)

    ql = ql_ref[...]
    qp = qp_ref[...]
    dn = (((1,), (1,)), ((), ()))

    sub = ch * page // KEY_SPLIT

    def attend(slot, carry):
        cks = [cbuf[slot, i * sub:(i + 1) * sub, :].astype(BF16) for i in range(KEY_SPLIT)]
        ss = [lax.dot_general(ql, cks[i], dn, preferred_element_type=F32)
              + _bdot(qp, kbuf[slot, :, i * sub:(i + 1) * sub].astype(BF16)) for i in range(KEY_SPLIT)]
        ms = [jnp.maximum(carry[i][0], jnp.max(ss[i], -1, keepdims=True)) for i in range(KEY_SPLIT)]
        ps = [jnp.exp2(ss[i] - ms[i]) for i in range(KEY_SPLIT)]
        pvs = [_bdot(ps[i].astype(BF16), cks[i]) for i in range(KEY_SPLIT)]
        new = []
        for i in range(KEY_SPLIT):
            m, l, acc = carry[i]
            alpha = jnp.exp2(m - ms[i])
            new.append((ms[i], alpha * l + jnp.sum(ps[i], -1, keepdims=True), alpha * acc + pvs[i]))
        return tuple(new)

    def ring(i, carry):
        for s in range(N_SLOTS):
            c = N_SLOTS * i + s
            wait(b, c, s)
            start_ahead(c, s)
            carry = attend(s, carry)
        return carry

    nh = ql.shape[0]
    one = (jnp.full((nh, 1), NEG, F32), jnp.zeros((nh, 1), F32), jnp.zeros((nh, ql.shape[1]), F32))
    states = lax.fori_loop(0, nch // N_SLOTS, ring, (one,) * KEY_SPLIT)

    cn = cn_ref[...]
    s_new = (jnp.sum(ql.astype(F32) * cn, -1, keepdims=True)
             + jnp.sum(qp.astype(F32) * kn_ref[...], -1, keepdims=True))
    m_f = s_new
    for m, _, _ in states:
        m_f = jnp.maximum(m_f, m)
    w_new = jnp.exp2(s_new - m_f)
    l_f = w_new
    acc_f = w_new * cn
    for m, l, acc in states:
        w = jnp.exp2(m - m_f)
        l_f = l_f + w * l
        acc_f = acc_f + w * acc
    o_ref[...] = acc_f / l_f


def _paged_attention(page_table, q_lat, q_pe, ckv_new, kpe_new, cache_ckv, cache_kpe_t):
    bsz, nh, kvl = q_lat.shape
    rope = q_pe.shape[-1]
    n_pages = page_table.shape[1]
    page = cache_ckv.shape[1]
    assert n_pages % (N_SLOTS * PAGES_PER_CHUNK) == 0 and (PAGES_PER_CHUNK * page) % (KEY_SPLIT * LANE) == 0
    rows = PAGES_PER_CHUNK * page
    blk = lambda a, b_: pl.BlockSpec((None, a, b_), lambda b, pt: (b, 0, 0))
    body = functools.partial(_paged_body, n_pages=n_pages, page=page)
    return pl.pallas_call(
        body,
        grid_spec=pltpu.PrefetchScalarGridSpec(
            num_scalar_prefetch=1, grid=(bsz,),
            in_specs=[blk(nh, kvl), blk(nh, rope), blk(1, kvl), blk(1, rope),
                      pl.BlockSpec(memory_space=pl.ANY), pl.BlockSpec(memory_space=pl.ANY)],
            out_specs=blk(nh, kvl),
            scratch_shapes=[pltpu.VMEM((N_SLOTS, rows, kvl), F32), pltpu.VMEM((N_SLOTS, rope, rows), F32),
                            pltpu.SemaphoreType.DMA((2, N_SLOTS))]),
        out_shape=jax.ShapeDtypeStruct((bsz, nh, kvl), F32),
        compiler_params=_cparams(("arbitrary",)),
        name="paged_attention",
    )(page_table, q_lat, q_pe, ckv_new, kpe_new, cache_ckv, cache_kpe_t)


def _rot_half_cols(w):
    half = w.shape[-1] // 2
    return jnp.concatenate([-w[..., half:], w[..., :half]], -1)


def _rope_tables(pos, rope, nope, scale):
    half = rope // 2
    inv = ROPE_THETA ** (-jnp.arange(half, dtype=F32) / half)
    ang = pos.astype(F32)[:, None] * inv[None, :]
    cos, sin = jnp.cos(ang), jnp.sin(ang)
    cos32 = jnp.concatenate([cos, cos], -1)
    sin32 = jnp.concatenate([sin, sin], -1)
    n = pos.shape[0]
    pad = HEAD_PAD - nope - rope
    cosq = jnp.concatenate([jnp.ones((n, nope), F32), cos32, jnp.zeros((n, pad), F32)], -1) * scale
    sinq = jnp.concatenate([jnp.zeros((n, nope), F32), sin32, jnp.zeros((n, pad), F32)], -1) * scale
    return cos32, sin32, cosq, sinq


def kernel(x_prompt, x_sample, c_prompt, c_sample, state_conv, cache_ckv, cache_kpe, page_table, ada_w, ada_b, norm_mix_g, norm_ffn_g, conv_w1, conv_b1, conv_wdw, conv_bdw, conv_ln_g, conv_ln_b, conv_w2, conv_b2, ada_kv_w, ada_kv_b, norm_kv_g, w_dkv, kv_norm_g, w_kr, w_uk, w_uv, w_dq, q_norm_g, w_uq, w_o, ffn_w_gate, ffn_w_up, ffn_w_down, router_w, router_b, moe_w_gate, moe_w_up, moe_w_down, final_norm_g):
    _, s, d = x_prompt.shape
    bd = x_sample.shape[0]
    assert x_prompt.shape[0] == 1 and x_sample.shape[1] == 1 and ada_w.shape[0] == 2
    kvl, n_heads, nope = w_uk.shape
    v_head = w_uv.shape[2]
    rope = w_kr.shape[1]
    q_lora = w_dq.shape[2]
    n_e = router_w.shape[2]
    page = cache_ckv.shape[1]
    past = page_table.shape[1] * page
    pad = HEAD_PAD - nope - rope
    sm_scale = float(nope + rope) ** -0.5 * LOG2_E

    w1 = conv_w1[0].astype(BF16)
    w2 = conv_w2[0].astype(BF16)
    wg0, wu0, wd0 = ffn_w_gate[0].astype(BF16), ffn_w_up[0].astype(BF16), ffn_w_down[0].astype(BF16)
    wge, wue, wde = moe_w_gate[0].astype(BF16), moe_w_up[0].astype(BF16), moe_w_down[0].astype(BF16)
    wo = w_o[0].astype(BF16)
    wr = jnp.pad(router_w[0], ((0, 0), (0, LANE - n_e)))
    br = jnp.pad(router_b[0], (0, LANE - n_e), constant_values=-jnp.inf).reshape(1, LANE)
    wuq = w_uq[0]
    wuq_rot = jnp.concatenate([jnp.zeros_like(wuq[..., :nope]), _rot_half_cols(wuq[..., nope:])], -1)
    pad_q = lambda w: jnp.pad(w, ((0, 0), (0, 0), (0, pad))).reshape(q_lora, n_heads * HEAD_PAD).astype(BF16)
    wk = jnp.pad(w_uk, ((0, 0), (0, 0), (0, HEAD_PAD - nope))).reshape(kvl, n_heads * HEAD_PAD).astype(BF16)
    ek = jnp.tile(jnp.pad(jnp.eye(rope, dtype=F32), ((0, 0), (nope, pad))), (1, n_heads)).astype(BF16)
    wuvt = jnp.pad(jnp.transpose(w_uv, (1, 2, 0)), ((0, 0), (0, V_ROWS - v_head), (0, 0)))
    wuvt = wuvt.reshape(n_heads * V_ROWS, kvl).astype(BF16)
    vone = jnp.tile((jnp.arange(V_ROWS) == v_head).astype(F32), n_heads).reshape(n_heads * V_ROWS, 1)
    kvq_w = dict(
        kg=norm_kv_g.reshape(1, d), wdkv=w_dkv.astype(BF16), kvg=kv_norm_g.reshape(1, kvl),
        wkr=w_kr.astype(BF16), wkrr=_rot_half_cols(w_kr).astype(BF16),
        qg=norm_mix_g[1].reshape(1, d), wdq=w_dq[0].astype(BF16), qng=q_norm_g[0].reshape(1, q_lora),
        wq=pad_q(wuq), wqr=pad_q(wuq_rot), wk=wk, ek=ek, wuvt=wuvt, vone=vone)
    wukt = jnp.pad(jnp.transpose(w_uk, (1, 2, 0)), ((0, 0), (0, HEAD_PAD - nope), (0, 0))).astype(BF16)
    wuv_h = jnp.transpose(w_uv, (1, 0, 2))
    wuvp = jnp.where((jnp.arange(n_heads) % 2 == 0)[:, None, None],
                     jnp.pad(wuv_h, ((0, 0), (0, 0), (0, v_head))),
                     jnp.pad(wuv_h, ((0, 0), (0, 0), (v_head, 0)))).astype(BF16)

    n_c = bd + 1
    c_all = jnp.pad(jnp.concatenate([c_sample, c_prompt], 0), ((0, -n_c % SUBLANE), (0, 0)))
    mods0 = _ada(c_all, ada_w, ada_b[0], layer=0)
    mods1 = _ada(c_all, ada_w, ada_b[1], layer=1)
    modskv = _ada(c_all, ada_kv_w, ada_kv_b)
    mp0, mp1, mpkv = mods0[bd:n_c], mods1[bd:n_c], modskv[bd:n_c]
    ms0, ms1, mskv = mods0[:bd], mods1[:bd], modskv[:bd]

    xp = x_prompt[0]
    x1, st_p = _mixer_prompt(xp, mp0, norm_mix_g[0], w1, conv_b1[0], conv_wdw[0], conv_bdw[0],
                             conv_ln_g[0], conv_ln_b[0], w2, conv_b2[0])
    x2 = _ffn(x1, mp0, norm_ffn_g[0], wg0, wu0, wd0, tm=1024)
    tabs_p = _rope_tables(jnp.arange(s, dtype=jnp.int32), rope, nope, sm_scale)
    ckv_p, kpe_p, q_p, k_p, vt_p = _kvq(x2, mpkv, mp1, kvq_w, tabs_p, tm=512, emit_kv=True)
    o_p = _flash_prompt(q_p, k_p, vt_p, v_head=v_head)
    y_p = _moe_routed(x2, o_p, wo, mp1, mp1, norm_ffn_g[1], (wr, br), wge, wue, wde, final_norm_g, tm=1024)

    xs = x_sample[:, 0]
    kw = conv_wdw.shape[1]
    state_t = jnp.transpose(state_conv[0], (1, 0, 2))
    xs1, u_s = _mixer_sample(xs, ms0, norm_mix_g[0], w1, conv_b1[0], state_t, conv_wdw[0], conv_bdw[0],
                             conv_ln_g[0], conv_ln_b[0], w2, conv_b2[0])
    xs2 = _ffn(xs1, ms0, norm_ffn_g[0], wg0, wu0, wd0, tm=bd)
    tabs_s = _rope_tables(jnp.full((1,), past, jnp.int32), rope, nope, sm_scale)
    ckv_s, kpe_s, q_s = _kvq(xs2, mskv, ms1, kvq_w, tabs_s, tm=bd, emit_kv=False)
    q_lat = _qlat(q_s, wukt).reshape(bd, n_heads, kvl)
    q_pe = q_s.reshape(bd, n_heads, HEAD_PAD)[:, :, nope:nope + rope]
    o_lat = _paged_attention(page_table, q_lat, q_pe, ckv_s.reshape(bd, 1, kvl), kpe_s.reshape(bd, 1, rope),
                             cache_ckv, jnp.transpose(cache_kpe, (0, 2, 1)))
    o_s = _ouv(o_lat.reshape(bd, n_heads * kvl), wuvp)
    y_s = _ffn(xs2, ms1, norm_ffn_g[1], wge, wue, wde, tm=bd, attn=(o_s, wo, ms1),
               router=(wr, br), final_g=final_norm_g)

    conv_state_prompt = st_p[CONV_HALO - (kw - 1):][None, None]
    conv_state_sample = jnp.concatenate([state_conv[0][:, 1:], u_s[:, None]], 1)[None]
    return (y_p[None], y_s[:, None], conv_state_prompt, conv_state_sample,
            ckv_p.reshape(1, s // page, page, kvl), kpe_p.reshape(1, s // page, page, rope),
            ckv_s[:, None], kpe_s[:, None])
```
